```python
import jax, jax.numpy as jnp
from jax import lax
import numpy as np

D_MODEL = 1024
BATCH = 32
SEQ = 2048
DEPTH = 1
DEC_BATCH = 128
DEC_SEQ = 8
PAST_LEN = 8192
PAGE_SIZE = 128

N_HEADS = 8
KV_HEADS = 2
HPG = N_HEADS // KV_HEADS
HEAD_DIM = 64
ATT_W = N_HEADS * HEAD_DIM
KV_W = KV_HEADS * HEAD_DIM
N_BRANCH = 3
L_CMP = 32
L_SEL = 64
CMP_PER_SEL = L_SEL // L_CMP
N_SEL = 16
N_FORCED = 3
N_TOP = N_SEL - N_FORCED
WINDOW = 512
Q_BLK = 128
SCALE = HEAD_DIM ** -0.5
ROT_DIM = HEAD_DIM // 4
ROPE_THETA = 500000.0
POOL_WINDOWS = (2, 4, 8, 16)
POOL_GROUPS = len(POOL_WINDOWS)
POOL_W = D_MODEL - ATT_W
POOL_GC = POOL_W // POOL_GROUPS
POOL_HIST = max(POOL_WINDOWS) - 1
MIX_W = ATT_W + POOL_W
OFF_KVC = ATT_W
OFF_KVS = OFF_KVC + 2 * KV_W
OFF_KVW = OFF_KVS + 2 * KV_W
OFF_GATE = OFF_KVW + 2 * KV_W
OFF_POOL = OFF_GATE + N_HEADS * N_BRANCH
IN_W = OFF_POOL + POOL_W
D_FF = -(-8 * D_MODEL // (3 * 256)) * 256
EPS = 1e-6
NEG = -1e30

kernel_name = 'nsa_pool_parallel_hybrid_step'


def rms_norm(x, g):
    xf = x.astype(jnp.float32)
    y = xf * lax.rsqrt(jnp.mean(xf * xf, axis=-1, keepdims=True) + EPS)
    return (y * g.astype(jnp.float32)).astype(x.dtype)


def rope(x, pos):
    half = ROT_DIM // 2
    inv = ROPE_THETA ** (-jnp.arange(half, dtype=jnp.float32) * 2.0 / ROT_DIM)
    ang = pos.astype(jnp.float32)[..., None] * inv
    cos = jnp.cos(ang)[..., None, :]
    sin = jnp.sin(ang)[..., None, :]
    xr = x[..., :ROT_DIM].astype(jnp.float32)
    x1, x2 = xr[..., :half], xr[..., half:]
    rot = jnp.concatenate([x1 * cos - x2 * sin, x2 * cos + x1 * sin], axis=-1).astype(x.dtype)
    return jnp.concatenate([rot, x[..., ROT_DIM:]], axis=-1)


def rope_kv(kv, pos):
    return jnp.stack([rope(kv[..., 0, :, :], pos), kv[..., 1, :, :]], axis=-3)


def masked_softmax(s, mask):
    s = jnp.where(mask, s.astype(jnp.float32), NEG)
    m = jnp.max(s, axis=-1, keepdims=True)
    e = jnp.exp(s - m) * mask
    return e / jnp.maximum(jnp.sum(e, axis=-1, keepdims=True), 1e-30)


def project(h, pos, w_in):
    B, T, _ = h.shape
    z = h @ w_in
    q = rope(z[..., :ATT_W].reshape(B, T, N_HEADS, HEAD_DIM), pos)
    q = q.reshape(B, T, KV_HEADS, HPG, HEAD_DIM)
    kv_c = z[..., OFF_KVC:OFF_KVS].reshape(B, T, 2, KV_HEADS, HEAD_DIM)
    kv_s = rope_kv(z[..., OFF_KVS:OFF_KVW].reshape(B, T, 2, KV_HEADS, HEAD_DIM), pos)
    kv_w = rope_kv(z[..., OFF_KVW:OFF_GATE].reshape(B, T, 2, KV_HEADS, HEAD_DIM), pos)
    gates = z[..., OFF_GATE:OFF_POOL].reshape(B, T, KV_HEADS, HPG, N_BRANCH)
    u = z[..., OFF_POOL:]
    return q, kv_c, kv_s, kv_w, gates, u


def compress_blocks(kv, cmp_pos, w_cmp):
    lead = kv.shape[:-4]
    n = kv.shape[-4] // L_CMP
    blk = kv.reshape(lead + (n, L_CMP, 2, KV_HEADS, HEAD_DIM))
    summ = jnp.einsum('...nlcgd,lcd->...ncgd', blk, cmp_pos)
    ckv = jnp.einsum('...ncgd,cde->...ncge', summ, w_cmp)
    end_pos = (jnp.arange(n, dtype=jnp.int32) + 1) * L_CMP - 1
    return rope(ckv[..., 0, :, :], end_pos), ckv[..., 1, :, :], end_pos


def cmp_attend(q, pos, ck, cv, end_pos):
    s = jnp.einsum('...tghd,...ngd->...tghn', q, ck) * SCALE
    mask = (end_pos[None, :] <= pos[:, None])[:, None, None, :]
    p = masked_softmax(s, mask)
    o = jnp.einsum('...tghn,...ngd->...tghd', p.astype(cv.dtype), cv)
    pc = jnp.sum(p, axis=-2)
    p_slc = pc.reshape(pc.shape[:-1] + (pc.shape[-1] // CMP_PER_SEL, CMP_PER_SEL)).sum(-1)
    return o, p_slc


def select_blocks(p_slc, pos):
    nb = p_slc.shape[-1]
    c = (pos // L_SEL)[:, None, None]
    j = jnp.arange(nb, dtype=jnp.int32)
    cand = (j >= 1) & (j <= c - 2)
    sc = jnp.where(cand, p_slc, -1.0)
    if nb < N_TOP:
        sc = jnp.pad(sc, [(0, 0)] * (sc.ndim - 1) + [(0, N_TOP - nb)], constant_values=-1.0)
    top_v, top_i = lax.top_k(sc, N_TOP)
    lead = top_i.shape[:-1]
    f_idx = jnp.broadcast_to(jnp.concatenate([jnp.zeros_like(c), c - 1, c], axis=-1), lead + (N_FORCED,))
    f_val = jnp.broadcast_to(jnp.concatenate([jnp.ones(c.shape, bool), c >= 2, c >= 1], axis=-1), lead + (N_FORCED,))
    idx = jnp.clip(jnp.concatenate([f_idx, top_i.astype(f_idx.dtype)], axis=-1), 0, nb - 1)
    valid = jnp.concatenate([f_val, top_v > -0.5], axis=-1)
    return idx, valid


def local_attend(q, pos, idx, valid, kv_g, kw, kw_pos):
    T = q.shape[0]
    n_key = N_SEL * L_SEL
    ks = kv_g[..., 0, :].reshape(T, KV_HEADS, n_key, HEAD_DIM)
    vs = kv_g[..., 1, :].reshape(T, KV_HEADS, n_key, HEAD_DIM)
    kpos = idx[..., None] * L_SEL + jnp.arange(L_SEL, dtype=jnp.int32)
    m_sel = (valid[..., None] & (kpos <= pos[:, None, None, None])).reshape(T, KV_HEADS, 1, n_key)
    s = jnp.einsum('tghd,tgkd->tghk', q, ks) * SCALE
    o_sel = jnp.einsum('tghk,tgkd->tghd', masked_softmax(s, m_sel).astype(vs.dtype), vs)
    d = pos[:, None] - kw_pos[None, :]
    m_win = ((d >= 0) & (d < WINDOW) & (kw_pos[None, :] >= 0))[:, None, None, :]
    s = jnp.einsum('tghd,kgd->tghk', q, kw[:, 0]) * SCALE
    o_win = jnp.einsum('tghk,kgd->tghd', masked_softmax(s, m_win).astype(kw.dtype), kw[:, 1])
    return o_sel, o_win


def nsa_prompt(q, kv_c, kv_s, kv_w, pos, cmp_pos, w_cmp):
    B, S = q.shape[:2]
    ck, cv, end_pos = compress_blocks(kv_c, cmp_pos, w_cmp)
    o_cmp, p_slc = cmp_attend(q, pos, ck, cv, end_pos)
    idx, valid = select_blocks(p_slc, pos)
    nqb = S // Q_BLK
    sel_blk = kv_s.reshape(B, S // L_SEL, L_SEL, 2, KV_HEADS, HEAD_DIM)
    kw_pad = jnp.pad(kv_w, ((0, 0), (WINDOW, 0), (0, 0), (0, 0), (0, 0)))
    g_ar = jnp.arange(KV_HEADS)[None, :, None]

    def item(args):
        b, qb, q_i, idx_i, val_i = args
        start = qb * Q_BLK
        pos_i = start + jnp.arange(Q_BLK, dtype=jnp.int32)
        kv_g = sel_blk[b, idx_i, :, :, g_ar]
        kw = lax.dynamic_slice(kw_pad, (b, start, 0, 0, 0), (1, WINDOW + Q_BLK, 2, KV_HEADS, HEAD_DIM))[0]
        kw_pos = start - WINDOW + jnp.arange(WINDOW + Q_BLK, dtype=jnp.int32)
        return local_attend(q_i, pos_i, idx_i, val_i, kv_g, kw, kw_pos)

    xs = (jnp.repeat(jnp.arange(B, dtype=jnp.int32), nqb),
          jnp.tile(jnp.arange(nqb, dtype=jnp.int32), B),
          q.reshape(B * nqb, Q_BLK, KV_HEADS, HPG, HEAD_DIM),
          idx.reshape(B * nqb, Q_BLK, KV_HEADS, N_SEL),
          valid.reshape(B * nqb, Q_BLK, KV_HEADS, N_SEL))
    o_sel, o_win = lax.map(item, xs)
    return o_cmp, o_sel.reshape(q.shape), o_win.reshape(q.shape)


def nsa_sample(q, kv_c, kv_s, kv_w, pos, cache_kv_cmp, cache_kv_sel, cache_kv_win, page_table, cmp_pos, w_cmp):
    T = q.shape[1]
    n_pages = page_table.shape[1]
    past = n_pages * PAGE_SIZE
    t_pad = -(-T // L_SEL) * L_SEL
    padw = ((0, 0), (0, t_pad - T), (0, 0), (0, 0), (0, 0))
    kvc_new = jnp.pad(kv_c, padw)
    kvs_new = jnp.pad(kv_s, padw)
    n_past_slc = past // L_SEL
    n_new_slc = t_pad // L_SEL
    bpp = PAGE_SIZE // L_SEL
    sel_pool = cache_kv_sel.reshape((-1, L_SEL, 2, KV_HEADS, HEAD_DIM))
    wb = cache_kv_win.shape[1]
    kw_all = jnp.concatenate([cache_kv_win.astype(kv_w.dtype), kv_w], axis=1)
    kw_pos = past - wb + jnp.arange(wb + T, dtype=jnp.int32)
    g_ar = jnp.arange(KV_HEADS)[None, :, None]

    def item(args):
        pt, q_b, kvc_b, kvs_b, kw_b = args
        past_c = cache_kv_cmp[pt].reshape(past, 2, KV_HEADS, HEAD_DIM).astype(kvc_b.dtype)
        ck, cv, end_pos = compress_blocks(jnp.concatenate([past_c, kvc_b], axis=0), cmp_pos, w_cmp)
        o_cmp, p_slc = cmp_attend(q_b, pos, ck, cv, end_pos)
        idx, valid = select_blocks(p_slc, pos)
        pidx = jnp.minimum(idx, n_past_slc - 1)
        phys = pt[pidx // bpp] * bpp + pidx % bpp
        g_past = sel_pool[phys, :, :, g_ar].astype(kvs_b.dtype)
        new_blk = kvs_b.reshape(n_new_slc, L_SEL, 2, KV_HEADS, HEAD_DIM)
        g_new = new_blk[jnp.clip(idx - n_past_slc, 0, n_new_slc - 1), :, :, g_ar]
        kv_g = jnp.where((idx < n_past_slc)[..., None, None, None], g_past, g_new)
        o_sel, o_win = local_attend(q_b, pos, idx, valid, kv_g, kw_b, kw_pos)
        return o_cmp, o_sel, o_win

    o_cmp, o_sel, o_win = lax.map(item, (page_table, q, kvc_new, kvs_new, kw_all))
    new_win = kw_all[:, -min(WINDOW, wb + T):]
    return o_cmp, o_sel, o_win, new_win


def pool_mix(u, hist, pos, w_pool, pool_scale):
    B, T, _ = u.shape
    z = jnp.concatenate([hist.astype(u.dtype), u], axis=1)
    cs = jnp.pad(jnp.cumsum(z.astype(jnp.float32), axis=1), ((0, 0), (1, 0), (0, 0)))
    uf = u.astype(jnp.float32)
    outs = []
    for gi, w in enumerate(POOL_WINDOWS):
        c0, c1 = gi * POOL_GC, (gi + 1) * POOL_GC
        win_sum = cs[:, POOL_HIST + 1:, c0:c1] - cs[:, POOL_HIST + 1 - w:POOL_HIST + 1 - w + T, c0:c1]
        cnt = jnp.minimum(pos + 1, w).astype(jnp.float32)[:, None]
        outs.append(win_sum / cnt - uf[..., c0:c1])
    d = jnp.stack(outs, axis=2).astype(u.dtype)
    y = jnp.einsum('btgc,gce->btge', d, w_pool).reshape(B, T, POOL_W) * pool_scale
    return y, z[:, -POOL_HIST:]


def merge(o_cmp, o_sel, o_win, gates, pool_y, w_out):
    g = jax.nn.sigmoid(gates.astype(jnp.float32))
    o = g[..., 0:1] * o_cmp + g[..., 1:2] * o_sel + g[..., 2:3] * o_win
    B, T = pool_y.shape[:2]
    mixed = jnp.concatenate([o.reshape(B, T, ATT_W).astype(pool_y.dtype), pool_y], axis=-1)
    return mixed @ w_out


def ffn_block(x, g_pre, g_post, w_gate, w_up, w_down):
    h = rms_norm(x, g_pre)
    f = (jax.nn.silu(h @ w_gate) * (h @ w_up)) @ w_down
    return x + rms_norm(f, g_post)


def setup_inputs(seed: int = 0) -> dict:
    key = jax.random.key(seed)
    ks = jax.random.split(key, 24)
    n_pages = PAST_LEN // PAGE_SIZE
    n_pool = (5 * DEC_BATCH * n_pages + 3) // 4
    win_buf = min(WINDOW, PAST_LEN)
    nrm = jax.random.normal
    f32 = jnp.float32
    page_table = jax.random.permutation(ks[7], n_pool)[:DEC_BATCH * n_pages]
    page_table = page_table.reshape(DEC_BATCH, n_pages).astype(jnp.int32)
    return {
        'x_prompt': nrm(ks[0], (BATCH, SEQ, D_MODEL), f32),
        'x_sample': nrm(ks[1], (DEC_BATCH, DEC_SEQ, D_MODEL), f32),
        'cache_kv_cmp': nrm(ks[2], (DEPTH, n_pool, PAGE_SIZE, 2, KV_HEADS, HEAD_DIM), f32),
        'cache_kv_sel': nrm(ks[3], (DEPTH, n_pool, PAGE_SIZE, 2, KV_HEADS, HEAD_DIM), f32),
        'cache_kv_win': nrm(ks[4], (DEPTH, DEC_BATCH, win_buf, 2, KV_HEADS, HEAD_DIM), f32),
        'state_pool': nrm(ks[5], (DEPTH, DEC_BATCH, POOL_HIST, POOL_W), f32),
        'page_table': page_table,
        'g_pre_mix': 1.0 + 0.1 * nrm(ks[8], (DEPTH, D_MODEL), f32),
        'g_post_mix': 1.0 + 0.1 * nrm(ks[9], (DEPTH, D_MODEL), f32),
        'g_pre_ffn': 1.0 + 0.1 * nrm(ks[10], (DEPTH, D_MODEL), f32),
        'g_post_ffn': 1.0 + 0.1 * nrm(ks[11], (DEPTH, D_MODEL), f32),
        'w_in': nrm(ks[12], (DEPTH, D_MODEL, IN_W), f32) * D_MODEL ** -0.5,
        'cmp_pos': (1.0 + 0.1 * nrm(ks[13], (DEPTH, L_CMP, 2, HEAD_DIM), f32)) / L_CMP,
        'w_cmp': nrm(ks[14], (DEPTH, 2, HEAD_DIM, HEAD_DIM), f32) * HEAD_DIM ** -0.5,
        'w_pool': nrm(ks[15], (DEPTH, POOL_GROUPS, POOL_GC, POOL_GC), f32) * POOL_GC ** -0.5,
        'pool_scale': 1.0 + 0.1 * nrm(ks[16], (DEPTH, POOL_W), f32),
        'w_out': nrm(ks[17], (DEPTH, MIX_W, D_MODEL), f32) * MIX_W ** -0.5,
        'w_gate': nrm(ks[18], (DEPTH, D_MODEL, D_FF), f32) * D_MODEL ** -0.5,
        'w_up': nrm(ks[19], (DEPTH, D_MODEL, D_FF), f32) * D_MODEL ** -0.5,
        'w_down': nrm(ks[20], (DEPTH, D_FF, D_MODEL), f32) * D_FF ** -0.5,
    }


def reference(x_prompt, x_sample, cache_kv_cmp, cache_kv_sel, cache_kv_win, state_pool, page_table,
              g_pre_mix, g_post_mix, g_pre_ffn, g_post_ffn, w_in, cmp_pos, w_cmp, w_pool, pool_scale,
              w_out, w_gate, w_up, w_down):
    B, S, _ = x_prompt.shape
    T = x_sample.shape[1]
    pos_p = jnp.arange(S, dtype=jnp.int32)
    pos_s = PAST_LEN + jnp.arange(T, dtype=jnp.int32)
    xp, xs = x_prompt, x_sample
    cmp_p, sel_p, win_p, pool_p = [], [], [], []
    cmp_s, sel_s, win_s, pool_s = [], [], [], []
    for l in range(DEPTH):
        h = rms_norm(xp, g_pre_mix[l])
        q, kvc, kvs, kvw, gates, u = project(h, pos_p, w_in[l])
        oc, osl, ow = nsa_prompt(q, kvc, kvs, kvw, pos_p, cmp_pos[l], w_cmp[l])
        py, pst = pool_mix(u, jnp.zeros((B, POOL_HIST, POOL_W), u.dtype), pos_p, w_pool[l], pool_scale[l])
        xp = xp + rms_norm(merge(oc, osl, ow, gates, py, w_out[l]), g_post_mix[l])
        xp = ffn_block(xp, g_pre_ffn[l], g_post_ffn[l], w_gate[l], w_up[l], w_down[l])
        cmp_p.append(kvc)
        sel_p.append(kvs)
        win_p.append(kvw[:, -min(WINDOW, S):])
        pool_p.append(pst)
        h = rms_norm(xs, g_pre_mix[l])
        q, kvc, kvs, kvw, gates, u = project(h, pos_s, w_in[l])
        oc, osl, ow, nwin = nsa_sample(q, kvc, kvs, kvw, pos_s, cache_kv_cmp[l], cache_kv_sel[l],
                                       cache_kv_win[l], page_table, cmp_pos[l], w_cmp[l])
        py, pst = pool_mix(u, state_pool[l], pos_s, w_pool[l], pool_scale[l])
        xs = xs + rms_norm(merge(oc, osl, ow, gates, py, w_out[l]), g_post_mix[l])
        xs = ffn_block(xs, g_pre_ffn[l], g_post_ffn[l], w_gate[l], w_up[l], w_down[l])
        cmp_s.append(kvc)
        sel_s.append(kvs)
        win_s.append(nwin)
        pool_s.append(pst)
    return (xp, xs, jnp.stack(cmp_p), jnp.stack(sel_p), jnp.stack(win_p), jnp.stack(pool_p),
            jnp.stack(cmp_s), jnp.stack(sel_s), jnp.stack(win_s), jnp.stack(pool_s))
```

```python
import functools

import jax
import jax.numpy as jnp
import numpy as np
from jax import lax
from jax.experimental import pallas as pl
from jax.experimental.pallas import tpu as pltpu

F32 = jnp.float32
MXU_DTYPE = jnp.bfloat16

D_MODEL = 1024
PAGE_SIZE = 128
N_HEADS = 8
KV_HEADS = 2
HPG = N_HEADS // KV_HEADS
HEAD_DIM = 64
ATT_W = N_HEADS * HEAD_DIM
KV_W = KV_HEADS * HEAD_DIM
N_BRANCH = 3
L_CMP = 32
L_SEL = 64
N_SEL = 16
N_FORCED = 3
N_TOP = N_SEL - N_FORCED
WINDOW = 512
SCALE = HEAD_DIM ** -0.5
ROT_DIM = HEAD_DIM // 4
ROT_HALF = ROT_DIM // 2
ROPE_THETA = 500000.0
POOL_WINDOWS = (2, 4, 8, 16)
POOL_W = D_MODEL - ATT_W
POOL_GC = POOL_W // len(POOL_WINDOWS)
POOL_HIST = max(POOL_WINDOWS) - 1
HIST_ROWS = POOL_HIST + 1
N_GATE = N_HEADS * N_BRANCH
D_FF = -(-8 * D_MODEL // (3 * 256)) * 256
EPS = 1e-6
NEG = -1e30

LANES = 128
C_Q = 0
C_KVC = C_Q + ATT_W
C_KVS = C_KVC + 2 * KV_W
C_KVW = C_KVS + 2 * KV_W
C_POOL = C_KVW + 2 * KV_W
C_GATE = C_POOL + POOL_W
IN_W_PAD = C_GATE + LANES

TM_IN = 512
TM_POST = 512
Q_BLK = 128
K_CHUNK = 256
W_CHUNK = 128
FF_CHUNKS = 2
VMEM_LIMIT = 56 * 1024 * 1024
SAMPLE_VMEM_LIMIT = 60 * 1024 * 1024


def _const_spec(shape):
    nd = len(shape)
    return pl.BlockSpec(shape, lambda *_: (0,) * nd, pipeline_mode=pl.Buffered(1))


def _rms(x, g):
    return x * lax.rsqrt(jnp.mean(x * x, axis=-1, keepdims=True) + EPS) * g


def _rope128(x, c, s1, s2):
    return x * c + pltpu.roll(x, ROT_HALF, 1) * s1 + pltpu.roll(x, LANES - ROT_HALF, 1) * s2


def _dot(a, b):
    return jnp.dot(a, b, preferred_element_type=F32)


def _dot_nt(a, b):
    return lax.dot_general(a, b, (((1,), (1,)), ((), ())), preferred_element_type=F32)


def _dot_tn(a, b):
    return lax.dot_general(a, b, (((0,), (0,)), ((), ())), preferred_element_type=F32)


def _rope_tables(pos):
    pos = jnp.asarray(np.asarray(pos), jnp.int32)
    inv = ROPE_THETA ** (-jnp.arange(ROT_HALF, dtype=F32) * 2.0 / ROT_DIM)
    ang = pos.astype(F32)[:, None] * inv
    cos, sin = jnp.cos(ang), jnp.sin(ang)
    n = pos.shape[0]
    one = jnp.ones((n, HEAD_DIM - ROT_DIM), F32)
    zero = jnp.zeros((n, HEAD_DIM - ROT_DIM), F32)
    zh = jnp.zeros((n, ROT_HALF), F32)
    c = jnp.concatenate([cos, cos, one], axis=1)
    s1 = jnp.concatenate([zh, sin, zero], axis=1)
    s2 = jnp.concatenate([-sin, zh, zero], axis=1)
    tile = lambda t: jnp.concatenate([t, t], axis=1)
    return tile(c), tile(s1), tile(s2)


def _prep_weights(w_in, cmp_pos, w_cmp, w_pool, pool_scale, w_out, w_gate, w_up, w_down):
    w_in_r = jnp.concatenate([
        w_in[:, :ATT_W + 6 * KV_W],
        w_in[:, ATT_W + 6 * KV_W + N_GATE:],
        w_in[:, ATT_W + 6 * KV_W:ATT_W + 6 * KV_W + N_GATE],
        jnp.zeros((D_MODEL, LANES - N_GATE), w_in.dtype)], axis=1).astype(MXU_DTYPE)
    cp = jnp.concatenate([cmp_pos[:, 0], cmp_pos[:, 0], cmp_pos[:, 1], cmp_pos[:, 1]], axis=1)
    wc = jnp.zeros((2 * KV_W, 2 * KV_W), F32)
    for i, c in enumerate((0, 0, 1, 1)):
        wc = wc.at[i * HEAD_DIM:(i + 1) * HEAD_DIM, i * HEAD_DIM:(i + 1) * HEAD_DIM].set(w_cmp[c])
    fc = D_FF // FF_CHUNKS
    return dict(
        w_in=w_in_r, cp=cp, wc=wc.astype(MXU_DTYPE), w_pool=w_pool.astype(MXU_DTYPE),
        pool_scale=pool_scale.reshape(1, POOL_W), w_out=w_out.astype(MXU_DTYPE),
        w_gate=w_gate.reshape(D_MODEL, FF_CHUNKS, fc).transpose(1, 0, 2).astype(MXU_DTYPE),
        w_up=w_up.reshape(D_MODEL, FF_CHUNKS, fc).transpose(1, 0, 2).astype(MXU_DTYPE),
        w_down=w_down.reshape(FF_CHUNKS, fc, D_MODEL).astype(MXU_DTYPE))


def _project_rows(x, g, w_ref, rc, rs1, rs2):
    h = _rms(x, g).astype(MXU_DTYPE)
    z = _dot(h, w_ref[...])
    q = jnp.concatenate(
        [_rope128(z[:, C_Q + i * LANES:C_Q + (i + 1) * LANES], rc, rs1, rs2) for i in range(ATT_W // LANES)],
        axis=1) * SCALE
    kvc = z[:, C_KVC:C_KVS]
    kvs = jnp.concatenate([_rope128(z[:, C_KVS:C_KVS + KV_W], rc, rs1, rs2), z[:, C_KVS + KV_W:C_KVW]], axis=1)
    kvw = jnp.concatenate([_rope128(z[:, C_KVW:C_KVW + KV_W], rc, rs1, rs2), z[:, C_KVW + KV_W:C_POOL]], axis=1)
    u = z[:, C_POOL:C_GATE]
    gates = z[:, C_GATE:IN_W_PAD]
    return q, kvc, kvs, kvw, u, gates


def _compress_rows(kvc, cp, wc_ref, cc, cs1, cs2):
    n = kvc.shape[0] // L_CMP
    summ = jnp.sum(kvc.reshape(n, L_CMP, 2 * KV_W) * cp[None], axis=1)
    ckv = _dot(summ.astype(MXU_DTYPE), wc_ref[...])
    return jnp.concatenate([_rope128(ckv[:, :KV_W], cc, cs1, cs2), ckv[:, KV_W:]], axis=1)


def _pool_windows(zf, rows, cnt):
    outs = []
    for gi, w in enumerate(POOL_WINDOWS):
        a = zf[:, gi * POOL_GC:(gi + 1) * POOL_GC]
        s, sh = a, 1
        while sh < w:
            s = s + pltpu.roll(s, sh, 0)
            sh *= 2
        outs.append(rows(s) / cnt[gi] - rows(a))
    return outs


def _pool_project(d_list, wp_ref, ps):
    y = jnp.concatenate([_dot(d.astype(MXU_DTYPE), wp_ref[gi]) for gi, d in enumerate(d_list)], axis=1)
    return y * ps


def _inproj_prompt_kernel(x_ref, g_ref, w_ref, rc_ref, rs1_ref, rs2_ref, cp_ref, wc_ref, cc_ref, cs1_ref, cs2_ref,
                          wp_ref, ps_ref,
                          q_ref, kvc_ref, kvs_ref, kvw_ref, kvsb_ref, kvwb_ref, gates_ref, pool_ref, ulast_ref,
                          ckv_ref, z_scr):
    j = pl.program_id(1)
    tm = x_ref.shape[0]
    nblk = tm // L_CMP
    r0 = pl.multiple_of(j * tm, tm)
    rows = pl.ds(r0, tm)
    q, kvc, kvs, kvw, u, gates = _project_rows(x_ref[...], g_ref[...], w_ref, rc_ref[rows, :], rs1_ref[rows, :],
                                               rs2_ref[rows, :])
    q_ref[...] = q.astype(q_ref.dtype)
    kvc_ref[...] = kvc
    kvs_ref[...] = kvs
    kvw_ref[...] = kvw
    kvsb_ref[...] = kvs.astype(kvsb_ref.dtype)
    kvwb_ref[...] = kvw.astype(kvwb_ref.dtype)
    gates_ref[...] = gates

    crow = pl.ds(pl.multiple_of(j * nblk, nblk), nblk)
    ckv_ref[...] = _compress_rows(kvc, cp_ref[...], wc_ref, cc_ref[crow, :], cs1_ref[crow, :], cs2_ref[crow, :])

    @pl.when(j == 0)
    def _():
        z_scr[0:HIST_ROWS, :] = jnp.zeros((HIST_ROWS, POOL_W), F32)

    @pl.when(j > 0)
    def _():
        z_scr[0:HIST_ROWS, :] = z_scr[tm:tm + HIST_ROWS, :]

    z_scr[HIST_ROWS:, :] = u
    pos = (r0 + lax.broadcasted_iota(jnp.int32, (tm, 1), 0)).astype(F32)
    cnt = [jnp.minimum(pos + 1.0, float(w)) for w in POOL_WINDOWS]
    d_list = _pool_windows(z_scr[...], lambda a: a[HIST_ROWS:, :], cnt)
    pool_ref[...] = _pool_project(d_list, wp_ref, ps_ref[...]).astype(pool_ref.dtype)
    ulast_ref[0] = u[tm - HIST_ROWS:, :]


def _inproj_prompt(x2, g_pre, wts, seq, batch):
    n = x2.shape[0]
    tm = min(TM_IN, seq)
    nj = seq // tm
    nblk = tm // L_CMP
    rc, rs1, rs2 = _rope_tables(np.arange(seq))
    cc, cs1, cs2 = _rope_tables((np.arange(seq // L_CMP) + 1) * L_CMP - 1)
    row = lambda w: pl.BlockSpec((tm, w), lambda b, j: (b * nj + j, 0))
    out_shape = (
        jax.ShapeDtypeStruct((n, ATT_W), MXU_DTYPE),
        jax.ShapeDtypeStruct((n, 2 * KV_W), F32),
        jax.ShapeDtypeStruct((n, 2 * KV_W), F32),
        jax.ShapeDtypeStruct((n, 2 * KV_W), F32),
        jax.ShapeDtypeStruct((n, 2 * KV_W), MXU_DTYPE),
        jax.ShapeDtypeStruct((n, 2 * KV_W), MXU_DTYPE),
        jax.ShapeDtypeStruct((n, LANES), F32),
        jax.ShapeDtypeStruct((n, POOL_W), MXU_DTYPE),
        jax.ShapeDtypeStruct((batch, HIST_ROWS, POOL_W), F32),
        jax.ShapeDtypeStruct((n // L_CMP, 2 * KV_W), F32),
    )
    out_specs = (row(ATT_W), row(2 * KV_W), row(2 * KV_W), row(2 * KV_W), row(2 * KV_W), row(2 * KV_W), row(LANES),
                 row(POOL_W), pl.BlockSpec((1, HIST_ROWS, POOL_W), lambda b, j: (b, 0, 0)),
                 pl.BlockSpec((nblk, 2 * KV_W), lambda b, j: (b * nj + j, 0)))
    in_specs = [row(D_MODEL), _const_spec((1, D_MODEL)), _const_spec((D_MODEL, IN_W_PAD)),
                _const_spec((seq, LANES)), _const_spec((seq, LANES)), _const_spec((seq, LANES)),
                _const_spec((L_CMP, 2 * KV_W)), _const_spec((2 * KV_W, 2 * KV_W)),
                _const_spec((seq // L_CMP, LANES)), _const_spec((seq // L_CMP, LANES)),
                _const_spec((seq // L_CMP, LANES)),
                _const_spec((len(POOL_WINDOWS), POOL_GC, POOL_GC)), _const_spec((1, POOL_W))]
    return pl.pallas_call(
        _inproj_prompt_kernel,
        grid=(batch, nj),
        in_specs=in_specs, out_specs=out_specs, out_shape=out_shape,
        scratch_shapes=[pltpu.VMEM((tm + HIST_ROWS, POOL_W), F32)],
        compiler_params=pltpu.CompilerParams(dimension_semantics=("arbitrary", "arbitrary"),
                                             vmem_limit_bytes=VMEM_LIMIT),
    )(x2, g_pre.reshape(1, D_MODEL), wts["w_in"], rc, rs1, rs2, wts["cp"], wts["wc"], cc, cs1, cs2,
      wts["w_pool"], wts["pool_scale"])


def _post_kernel(x_ref, o_ref, p_ref, wo_ref, gpm_ref, gpf_ref, gqf_ref, wg_ref, wu_ref, wd_ref, y_ref):
    mixed = _dot(jnp.concatenate([o_ref[...].astype(MXU_DTYPE), p_ref[...]], axis=1), wo_ref[...])
    x1 = x_ref[...] + _rms(mixed, gpm_ref[...])
    h2 = _rms(x1, gpf_ref[...]).astype(MXU_DTYPE)
    acc = None
    for c in range(FF_CHUNKS):
        f = (jax.nn.silu(_dot(h2, wg_ref[c])) * _dot(h2, wu_ref[c])).astype(MXU_DTYPE)
        part = _dot(f, wd_ref[c])
        acc = part if acc is None else acc + part
    y_ref[...] = x1 + _rms(acc, gqf_ref[...])


def _post(x2, o_att, pool_y, g_post_mix, g_pre_ffn, g_post_ffn, wts):
    n = x2.shape[0]
    tm = min(TM_POST, n)
    fc = D_FF // FF_CHUNKS
    row = lambda w: pl.BlockSpec((tm, w), lambda i: (i, 0))
    return pl.pallas_call(
        _post_kernel,
        grid=(n // tm,),
        in_specs=[row(D_MODEL), row(ATT_W), row(POOL_W), _const_spec((D_MODEL, D_MODEL)),
                  _const_spec((1, D_MODEL)), _const_spec((1, D_MODEL)), _const_spec((1, D_MODEL)),
                  _const_spec((FF_CHUNKS, D_MODEL, fc)), _const_spec((FF_CHUNKS, D_MODEL, fc)),
                  _const_spec((FF_CHUNKS, fc, D_MODEL))],
        out_specs=row(D_MODEL),
        out_shape=jax.ShapeDtypeStruct((n, D_MODEL), F32),
        compiler_params=pltpu.CompilerParams(dimension_semantics=("arbitrary",), vmem_limit_bytes=VMEM_LIMIT),
    )(x2, o_att, pool_y, wts["w_out"], g_post_mix.reshape(1, D_MODEL), g_pre_ffn.reshape(1, D_MODEL),
      g_post_ffn.reshape(1, D_MODEL), wts["w_gate"], wts["w_up"], wts["w_down"])


def _softmax_update(carry, s, ok, v):
    m, l, acc = carry
    s = jnp.where(ok, s, NEG)
    mn = jnp.maximum(m, jnp.max(s, axis=1, keepdims=True))
    p = jnp.where(ok, jnp.exp(s - mn), 0.0)
    a = jnp.exp(m - mn)
    l = a * l + jnp.sum(p, axis=1, keepdims=True)
    acc = a * acc + _dot(p.astype(MXU_DTYPE), v)
    return mn, l, acc


def _select_mask(p_slc, c, n_top):
    nb = p_slc.shape[1]
    jj = lax.broadcasted_iota(jnp.int32, (1, nb), 1)
    cand = (jj >= 1) & (jj <= c - 2)
    sc = jnp.where(cand, p_slc, -1.0)
    rank = jnp.zeros(sc.shape, F32)
    for i in range(nb):
        col = sc[:, i:i + 1]
        beats = (col > sc) | ((col == sc) & (jj > i))
        rank = rank + jnp.where(beats, 1.0, 0.0)
    return (cand & (rank < n_top)) | (jj == 0) | (jj == c) | (jj == c - 1)


def _attn_prompt_kernel(q_ref, gt_ref, ckv_ref, ks_ref, kw_ref, e_ref, o_ref):
    qb = pl.program_id(1)
    qblk = q_ref.shape[0]
    rows = HPG * qblk
    ncb = ckv_ref.shape[1]
    start = qb * qblk
    tpos = start + lax.broadcasted_iota(jnp.int32, (qblk, 1), 0)
    tpos4 = jnp.concatenate([tpos] * HPG, axis=0)
    q = q_ref[...]
    gs = jax.nn.sigmoid(gt_ref[...])
    ckv = ckv_ref[0]
    lane = lax.broadcasted_iota(jnp.int32, (1, ncb), 1)
    half = ncb // 2
    cblk = 2 * (lane % half) + lane // half
    cmask = ((cblk + 1) * L_CMP - 1) <= tpos4
    init = (jnp.full((rows, 1), NEG, F32), jnp.zeros((rows, 1), F32), jnp.zeros((rows, HEAD_DIM), F32))
    outs = []
    for g in range(KV_HEADS):
        kcol = slice(g * HEAD_DIM, (g + 1) * HEAD_DIM)
        vcol = slice(KV_W + g * HEAD_DIM, KV_W + (g + 1) * HEAD_DIM)
        qg = jnp.concatenate(
            [q[:, (g * HPG + h) * HEAD_DIM:(g * HPG + h + 1) * HEAD_DIM] for h in range(HPG)], axis=0)

        s = jnp.where(cmask, _dot_nt(qg, ckv[:, kcol].astype(MXU_DTYPE)), NEG)
        e = jnp.where(cmask, jnp.exp(s - jnp.max(s, axis=1, keepdims=True)), 0.0)
        p = e / jnp.maximum(jnp.sum(e, axis=1, keepdims=True), 1e-30)
        o_c = _dot(p.astype(MXU_DTYPE), ckv[:, vcol].astype(MXU_DTYPE))
        pc = p[0:qblk]
        for h in range(1, HPG):
            pc = pc + p[h * qblk:(h + 1) * qblk]
        p_slc = pc[:, :half] + pc[:, half:]

        sel = _select_mask(p_slc, tpos // L_SEL, N_TOP)
        selm = jnp.where(sel, 1.0, 0.0).astype(MXU_DTYPE)
        selm4 = jnp.concatenate([selm] * HPG, axis=0)

        def sel_body(ci, carry):
            k0 = pl.multiple_of(ci * K_CHUNK, K_CHUNK)
            kc = ks_ref[0, pl.ds(k0, K_CHUNK), kcol]
            vc = ks_ref[0, pl.ds(k0, K_CHUNK), vcol]
            kpos = k0 + lax.broadcasted_iota(jnp.int32, (1, K_CHUNK), 1)
            ok = (_dot(selm4, e_ref[ci]) > 0.5) & (kpos <= tpos4)
            return _softmax_update(carry, _dot_nt(qg, kc), ok, vc)

        n_sel_chunks = (start + qblk + K_CHUNK - 1) // K_CHUNK
        _, l, acc = lax.fori_loop(0, n_sel_chunks, sel_body, init)
        o_s = acc / jnp.maximum(l, 1e-30)

        def win_body(ci, carry):
            k0 = pl.multiple_of(ci * W_CHUNK, W_CHUNK)
            kc = kw_ref[0, pl.ds(k0, W_CHUNK), kcol]
            vc = kw_ref[0, pl.ds(k0, W_CHUNK), vcol]
            d = tpos4 - (k0 + lax.broadcasted_iota(jnp.int32, (1, W_CHUNK), 1))
            ok = (d >= 0) & (d < WINDOW)
            return _softmax_update(carry, _dot_nt(qg, kc), ok, vc)

        w_lo = jnp.maximum(start - WINDOW, 0) // W_CHUNK
        w_hi = (start + qblk + W_CHUNK - 1) // W_CHUNK
        _, l, acc = lax.fori_loop(w_lo, w_hi, win_body, init)
        o_w = acc / jnp.maximum(l, 1e-30)

        for h in range(HPG):
            col = (g * HPG + h) * N_BRANCH
            sl = slice(h * qblk, (h + 1) * qblk)
            outs.append(gs[:, col:col + 1] * o_c[sl] + gs[:, col + 1:col + 2] * o_s[sl]
                        + gs[:, col + 2:col + 3] * o_w[sl])
    o_ref[...] = jnp.concatenate(outs, axis=1).astype(o_ref.dtype)


def _attn_prompt(q, gates, ckv_perm, kvsb, kvwb, seq, batch):
    n = q.shape[0]
    qblk = min(Q_BLK, seq)
    nqb = seq // qblk
    nb = seq // L_SEL
    nchunk = seq // K_CHUNK
    e = (np.arange(seq)[None, :] // L_SEL == np.arange(nb)[:, None]).astype(np.float32)
    e = jnp.asarray(e.reshape(nb, nchunk, K_CHUNK).transpose(1, 0, 2), MXU_DTYPE)
    row = lambda w: pl.BlockSpec((qblk, w), lambda b, i: (b * nqb + i, 0))
    per_b = lambda r, w: pl.BlockSpec((1, r, w), lambda b, i: (b, 0, 0))
    return pl.pallas_call(
        _attn_prompt_kernel,
        grid=(batch, nqb),
        in_specs=[row(ATT_W), row(LANES), per_b(seq // L_CMP, 2 * KV_W), per_b(seq, 2 * KV_W),
                  per_b(seq, 2 * KV_W), _const_spec((nchunk, nb, K_CHUNK))],
        out_specs=row(ATT_W),
        out_shape=jax.ShapeDtypeStruct((n, ATT_W), MXU_DTYPE),
        compiler_params=pltpu.CompilerParams(dimension_semantics=("arbitrary", "arbitrary"),
                                             vmem_limit_bytes=VMEM_LIMIT),
    )(q, gates, ckv_perm, kvsb.reshape(batch, seq, 2 * KV_W), kvwb.reshape(batch, seq, 2 * KV_W), e)


def _prompt_group(x_prompt, g_pre_mix, g_post_mix, g_pre_ffn, g_post_ffn, wts):
    batch, seq, _ = x_prompt.shape
    x2 = x_prompt.reshape(batch * seq, D_MODEL)
    q, kvc, kvs, kvw, kvsb, kvwb, gates, pool_y, ulast, ckv = _inproj_prompt(x2, g_pre_mix, wts, seq, batch)
    ncb = seq // L_CMP
    ckv_perm = ckv.reshape(batch, ncb // 2, 2, 2 * KV_W).transpose(0, 2, 1, 3).reshape(batch, ncb, 2 * KV_W)
    o_att = _attn_prompt(q, gates, ckv_perm, kvsb, kvwb, seq, batch)
    y = _post(x2, o_att, pool_y, g_post_mix, g_pre_ffn, g_post_ffn, wts)
    kv6 = lambda a: a.reshape(1, batch, seq, 2, KV_HEADS, HEAD_DIM)
    wkeep = min(WINDOW, seq)
    return (y.reshape(batch, seq, D_MODEL), kv6(kvc), kv6(kvs), kv6(kvw)[:, :, seq - wkeep:],
            ulast[None, :, 1:, :])


def _inproj_sample_kernel(x_ref, g_ref, w_ref, rc_ref, rs1_ref, rs2_ref, cp_ref, wc_ref, cc_ref, cs1_ref, cs2_ref,
                          wp_ref, ps_ref, hist_ref,
                          q_ref, kvc_ref, kvs_ref, kvw_ref, gates_ref, pool_ref, znew_ref, ckv_ref, z_scr, *, past):
    nseq, zrows, _ = z_scr.shape
    t_new = zrows - HIST_ROWS
    n = nseq * t_new
    q, kvc, kvs, kvw, u, gates = _project_rows(x_ref[...], g_ref[...], w_ref, rc_ref[...], rs1_ref[...],
                                               rs2_ref[...])
    q_ref[...] = q
    kvc_ref[...] = kvc
    kvs_ref[...] = kvs
    kvw_ref[...] = kvw
    gates_ref[...] = gates

    summ = jnp.sum(kvc.reshape(nseq, t_new, 2 * KV_W) * cp_ref[0:t_new, :][None], axis=1)
    ckv = _dot(summ.astype(MXU_DTYPE), wc_ref[...])
    ckv_ref[...] = jnp.concatenate([_rope128(ckv[:, :KV_W], cc_ref[...], cs1_ref[...], cs2_ref[...]),
                                    ckv[:, KV_W:]], axis=1)

    z_scr[:, 0:HIST_ROWS, :] = hist_ref[...]
    z_scr[:, HIST_ROWS:, :] = u.reshape(nseq, t_new, POOL_W)
    zf = z_scr[...].reshape(nseq * zrows, POOL_W)
    tok = lax.broadcasted_iota(jnp.int32, (n, 1), 0) % t_new
    pos = (past + tok).astype(F32)
    cnt = [jnp.minimum(pos + 1.0, float(w)) for w in POOL_WINDOWS]
    take = lambda a: a.reshape(nseq, zrows, POOL_GC)[:, HIST_ROWS:, :].reshape(n, POOL_GC)
    d_list = _pool_windows(zf, take, cnt)
    pool_ref[...] = _pool_project(d_list, wp_ref, ps_ref[...]).astype(pool_ref.dtype)
    znew_ref[...] = z_scr[:, zrows - HIST_ROWS:, :]


def _inproj_sample(x2, g_pre, wts, hist16, nseq, t_new, past):
    n = x2.shape[0]
    rc, rs1, rs2 = _rope_tables(np.tile(past + np.arange(t_new), nseq))
    cc, cs1, cs2 = _rope_tables(np.array([past + L_CMP - 1]))
    full = lambda *s: pl.BlockSpec(s, lambda i: (0,) * len(s))
    out_shape = (
        jax.ShapeDtypeStruct((n, ATT_W), F32),
        jax.ShapeDtypeStruct((n, 2 * KV_W), F32),
        jax.ShapeDtypeStruct((n, 2 * KV_W), F32),
        jax.ShapeDtypeStruct((n, 2 * KV_W), F32),
        jax.ShapeDtypeStruct((n, LANES), F32),
        jax.ShapeDtypeStruct((n, POOL_W), MXU_DTYPE),
        jax.ShapeDtypeStruct((nseq, HIST_ROWS, POOL_W), F32),
        jax.ShapeDtypeStruct((nseq, 2 * KV_W), F32),
    )
    out_specs = (full(n, ATT_W), full(n, 2 * KV_W), full(n, 2 * KV_W), full(n, 2 * KV_W), full(n, LANES),
                 full(n, POOL_W), full(nseq, HIST_ROWS, POOL_W), full(nseq, 2 * KV_W))
    in_specs = [full(n, D_MODEL), full(1, D_MODEL), full(D_MODEL, IN_W_PAD), full(n, LANES), full(n, LANES),
                full(n, LANES), full(L_CMP, 2 * KV_W), full(2 * KV_W, 2 * KV_W), full(1, LANES), full(1, LANES),
                full(1, LANES), full(len(POOL_WINDOWS), POOL_GC, POOL_GC), full(1, POOL_W),
                full(nseq, HIST_ROWS, POOL_W)]
    return pl.pallas_call(
        functools.partial(_inproj_sample_kernel, past=past),
        grid=(1,),
        in_specs=in_specs, out_specs=out_specs, out_shape=out_shape,
        scratch_shapes=[pltpu.VMEM((nseq, HIST_ROWS + t_new, POOL_W), F32)],
        compiler_params=pltpu.CompilerParams(dimension_semantics=("arbitrary",), vmem_limit_bytes=VMEM_LIMIT),
    )(x2, g_pre.reshape(1, D_MODEL), wts["w_in"], rc, rs1, rs2, wts["cp"], wts["wc"], cc, cs1, cs2,
      wts["w_pool"], wts["pool_scale"], hist16)


S_CHUNK = 1024
C_CHUNK = 1024


def _sample_lane_maps(t_new):
    lam = np.arange(LANES) % (HPG * KV_HEADS * t_new)
    return lam // (KV_HEADS * t_new), (lam // t_new) % KV_HEADS, lam % t_new


def _sample_masks(past, t_new, wb, ncb_pad, nb_pad):
    _, _, t = _sample_lane_maps(t_new)
    pos = past + t
    n_cmp = (past + -(-t_new // L_SEL) * L_SEL) // L_CMP
    nb = n_cmp // (L_SEL // L_CMP)
    n = np.arange(ncb_pad)[:, None]
    cmask = (n < n_cmp) & ((n + 1) * L_CMP - 1 <= pos[None, :])
    j = np.arange(nb_pad)[:, None]
    c = (pos // L_SEL)[None, :]
    cand = (j >= 1) & (j <= c - 2) & (j < nb)
    forced = ((j == 0) | (j == c) | (j == c - 1)) & (j < nb)
    i = np.arange(wb + 2 * t_new)[:, None]
    d = t[None, :] + wb - i
    wmask = (d >= 0) & (d < WINDOW) & (past - wb + i >= 0) & (i < wb + t_new)
    l = np.arange(2 * t_new)[:, None]
    nmask = (l <= t[None, :]) & (l < t_new)
    f = lambda a: jnp.asarray(a.astype(np.float32))
    return f(cmask), f(cand), f(forced), f(wmask), f(nmask), nb


def _attn_sample_kernel(pt_ref, q_ref, gt_ref, kvs_ref, kvw_ref, ckv0_ref, cw_ref, cmp_hbm, sel_hbm,
                        cp_ref, wc_ref, cc_ref, cs1_ref, cs2_ref, cmask_ref, cand_ref, forced_ref, wmask_ref,
                        nmask_ref, eexp_ref,
                        o_ref, wout_ref,
                        cbuf, sbuf, sem, ckv_scr, pc_scr, sc_scr, mask_scr, s_scr, *, nb, n_top):
    b = pl.program_id(0)
    nseq = pl.num_programs(0)
    npages = pt_ref.shape[1]
    past = npages * PAGE_SIZE
    t_new = q_ref.shape[1]
    wb = cw_ref.shape[1]
    ncb_pad = ckv_scr.shape[0]
    nb_pad = mask_scr.shape[0]
    slot = b % 2

    def copies(seq, sl):
        out = []
        for p in range(npages):
            pg = pt_ref[seq, p]
            dst = pl.ds(p * PAGE_SIZE, PAGE_SIZE)
            out.append(pltpu.make_async_copy(cmp_hbm.at[pg], cbuf.at[sl, dst], sem.at[0, sl]))
            out.append(pltpu.make_async_copy(sel_hbm.at[pg], sbuf.at[sl, dst], sem.at[1, sl]))
        return out

    @pl.when(b == 0)
    def _():
        for cpy in copies(b, slot):
            cpy.start()

    @pl.when(b + 1 < nseq)
    def _():
        for cpy in copies(b + 1, 1 - slot):
            cpy.start()

    qb = q_ref[0]
    zeros = jnp.zeros((t_new, HEAD_DIM), F32)
    blocks = []
    for h in range(HPG):
        for g in range(KV_HEADS):
            piece = qb[:, (g * HPG + h) * HEAD_DIM:(g * HPG + h + 1) * HEAD_DIM]
            blocks.append(jnp.concatenate([piece, zeros] if g == 0 else [zeros, piece], axis=1))
    qbd = jnp.concatenate(blocks + blocks, axis=0).astype(MXU_DTYPE)

    kw_all = jnp.concatenate([cw_ref[0], kvw_ref[0], jnp.zeros((t_new, 2 * KV_W), F32)], axis=0)
    wok = wmask_ref[...] > 0.5
    s = jnp.where(wok, _dot_nt(kw_all[:, :KV_W].astype(MXU_DTYPE), qbd), NEG)
    e = jnp.where(wok, jnp.exp(s - jnp.max(s, axis=0, keepdims=True)), 0.0)
    p = e / jnp.maximum(jnp.sum(e, axis=0, keepdims=True), 1e-30)
    o_w = _dot_tn(p.astype(MXU_DTYPE), kw_all[:, KV_W:].astype(MXU_DTYPE))
    wout_ref[0, 0:wb - t_new, :] = cw_ref[0, t_new:wb, :]
    wout_ref[0, wb - t_new:wb, :] = kvw_ref[0]

    for cpy in copies(b, slot):
        cpy.wait()

    nper = C_CHUNK // L_CMP

    def cmp_body(ci, _):
        r0 = pl.multiple_of(ci * C_CHUNK, C_CHUNK)
        blk = cbuf[slot, pl.ds(r0, C_CHUNK), :].reshape(nper, L_CMP, 2 * KV_W)
        ckv_scr[pl.ds(pl.multiple_of(ci * nper, nper), nper), :] = jnp.sum(blk * cp_ref[...][None], axis=1)
        return 0

    lax.fori_loop(0, past // C_CHUNK, cmp_body, 0)
    n_past_cmp = past // L_CMP
    ckv_past = _dot(ckv_scr[0:n_past_cmp, :].astype(MXU_DTYPE), wc_ref[...])
    ckv_past = jnp.concatenate([_rope128(ckv_past[:, :KV_W], cc_ref[...], cs1_ref[...], cs2_ref[...]),
                                ckv_past[:, KV_W:]], axis=1)
    pad = jnp.zeros((ncb_pad - n_past_cmp - 8, 2 * KV_W), F32)
    new8 = jnp.concatenate([ckv0_ref[0], jnp.zeros((7, 2 * KV_W), F32)], axis=0)
    ckv_all = jnp.concatenate([ckv_past, new8, pad], axis=0)

    cok = cmask_ref[...] > 0.5
    s = jnp.where(cok, _dot_nt(ckv_all[:, :KV_W].astype(MXU_DTYPE), qbd), NEG)
    e = jnp.where(cok, jnp.exp(s - jnp.max(s, axis=0, keepdims=True)), 0.0)
    p = e / jnp.maximum(jnp.sum(e, axis=0, keepdims=True), 1e-30)
    o_c = _dot_tn(p.astype(MXU_DTYPE), ckv_all[:, KV_W:].astype(MXU_DTYPE))
    hstep = KV_HEADS * t_new
    pc = p
    for h in range(1, HPG):
        pc = pc + pltpu.roll(p, h * hstep, 1)
    pc_scr[...] = pc
    p_slc = pc_scr[pl.ds(0, nb_pad, stride=2), :] + pc_scr[pl.ds(1, nb_pad, stride=2), :]

    cand = cand_ref[...] > 0.5
    sc = jnp.where(cand, p_slc, -1.0)
    sc_scr[...] = sc
    jrow = lax.broadcasted_iota(jnp.int32, (nb_pad, LANES), 0)

    def rank_body(i, rank):
        row = sc_scr[pl.ds(i, 1), :]
        beats = (row > sc) | ((row == sc) & (jrow > i))
        return rank + jnp.where(beats, 1.0, 0.0)

    rank = lax.fori_loop(0, nb, rank_body, jnp.zeros((nb_pad, LANES), F32))
    sel = (cand & (rank < n_top)) | (forced_ref[...] > 0.5)
    mask_scr[...] = jnp.where(sel, 1.0, 0.0)

    bpc = S_CHUNK // L_SEL
    n_chunks = past // S_CHUNK

    def score_body(ci, m):
        r0 = pl.multiple_of(ci * S_CHUNK, S_CHUNK)
        kc = sbuf[slot, pl.ds(r0, S_CHUNK), 0:KV_W].astype(MXU_DTYPE)
        mrows = mask_scr[pl.ds(pl.multiple_of(ci * bpc, bpc), bpc), :].astype(MXU_DTYPE)
        ok = _dot(eexp_ref[...], mrows) > 0.5
        sv = jnp.where(ok, _dot_nt(kc, qbd), NEG)
        s_scr[pl.ds(r0, S_CHUNK), :] = sv
        return jnp.maximum(m, jnp.max(sv, axis=0, keepdims=True))

    m = lax.fori_loop(0, n_chunks, score_body, jnp.full((1, LANES), NEG, F32))
    ks_new = jnp.concatenate([kvs_ref[0], jnp.zeros((t_new, 2 * KV_W), F32)], axis=0)
    nblk_new = past // L_SEL
    nok = (nmask_ref[...] > 0.5) & (mask_scr[nblk_new:nblk_new + 1, :] > 0.5)
    s_new = jnp.where(nok, _dot_nt(ks_new[:, :KV_W].astype(MXU_DTYPE), qbd), NEG)
    m = jnp.maximum(m, jnp.max(s_new, axis=0, keepdims=True))

    def exp_body(ci, l):
        r0 = pl.multiple_of(ci * S_CHUNK, S_CHUNK)
        sv = s_scr[pl.ds(r0, S_CHUNK), :]
        ev = jnp.where(sv > 0.5 * NEG, jnp.exp(sv - m), 0.0)
        s_scr[pl.ds(r0, S_CHUNK), :] = ev
        return l + jnp.sum(ev, axis=0, keepdims=True)

    e_new = jnp.where(nok, jnp.exp(s_new - m), 0.0)
    l = lax.fori_loop(0, n_chunks, exp_body, jnp.sum(e_new, axis=0, keepdims=True))
    l = jnp.maximum(l, 1e-30)

    def pv_body(ci, acc):
        r0 = pl.multiple_of(ci * S_CHUNK, S_CHUNK)
        pv = (s_scr[pl.ds(r0, S_CHUNK), :] / l).astype(MXU_DTYPE)
        vc = sbuf[slot, pl.ds(r0, S_CHUNK), KV_W:2 * KV_W].astype(MXU_DTYPE)
        return acc + _dot_tn(pv, vc)

    acc0 = _dot_tn((e_new / l).astype(MXU_DTYPE), ks_new[:, KV_W:].astype(MXU_DTYPE))
    o_s = lax.fori_loop(0, n_chunks, pv_body, acc0)

    gs = jax.nn.sigmoid(gt_ref[0])
    outs = [None] * N_HEADS
    for h in range(HPG):
        for g in range(KV_HEADS):
            hh = g * HPG + h
            rs = slice((h * KV_HEADS + g) * t_new, (h * KV_HEADS + g + 1) * t_new)
            cs = slice(g * HEAD_DIM, (g + 1) * HEAD_DIM)
            col = hh * N_BRANCH
            outs[hh] = (gs[:, col:col + 1] * o_c[rs, cs] + gs[:, col + 1:col + 2] * o_s[rs, cs]
                        + gs[:, col + 2:col + 3] * o_w[rs, cs])
    o_ref[0] = jnp.concatenate(outs, axis=1)


def _attn_sample(q, gates, kvs, kvw, ckv0, cache_cmp, cache_sel, cache_win, page_table, wts, t_new):
    nseq, npages = page_table.shape
    past = npages * PAGE_SIZE
    wb = cache_win.shape[1]
    assert t_new <= L_CMP and HPG * KV_HEADS * t_new * 2 == LANES and wb + t_new >= WINDOW
    assert past % S_CHUNK == 0 and past % C_CHUNK == 0
    n_cmp = (past + L_SEL) // L_CMP
    nb_pad = -(-(n_cmp // 2) // 8) * 8
    ncb_pad = 2 * nb_pad
    cmask, cand, forced, wmask, nmask, nb = _sample_masks(past, t_new, wb, ncb_pad, nb_pad)
    cc, cs1, cs2 = _rope_tables((np.arange(past // L_CMP) + 1) * L_CMP - 1)
    bpc = S_CHUNK // L_SEL
    eexp = jnp.asarray(np.arange(S_CHUNK)[:, None] // L_SEL == np.arange(bpc)[None, :], MXU_DTYPE)
    n_pool = cache_cmp.shape[0]
    seq3 = lambda r, w: pl.BlockSpec((1, r, w), lambda b, pt: (b, 0, 0))
    const = lambda *s: pl.BlockSpec(s, lambda b, pt: (0,) * len(s))
    hbm = pl.BlockSpec(memory_space=pl.ANY)
    in_specs = [seq3(t_new, ATT_W), seq3(t_new, LANES), seq3(t_new, 2 * KV_W), seq3(t_new, 2 * KV_W),
                seq3(1, 2 * KV_W), seq3(wb, 2 * KV_W), hbm, hbm,
                const(L_CMP, 2 * KV_W), const(2 * KV_W, 2 * KV_W), const(past // L_CMP, LANES),
                const(past // L_CMP, LANES), const(past // L_CMP, LANES),
                const(ncb_pad, LANES), const(nb_pad, LANES), const(nb_pad, LANES), const(wb + 2 * t_new, LANES),
                const(2 * t_new, LANES), const(S_CHUNK, bpc)]
    grid_spec = pltpu.PrefetchScalarGridSpec(
        num_scalar_prefetch=1, grid=(nseq,), in_specs=in_specs,
        out_specs=(seq3(t_new, ATT_W), seq3(wb, 2 * KV_W)),
        scratch_shapes=[pltpu.VMEM((2, past, 2 * KV_W), F32), pltpu.VMEM((2, past, 2 * KV_W), F32),
                        pltpu.SemaphoreType.DMA((2, 2)),
                        pltpu.VMEM((ncb_pad, 2 * KV_W), F32), pltpu.VMEM((ncb_pad, LANES), F32),
                        pltpu.VMEM((nb_pad, LANES), F32), pltpu.VMEM((nb_pad, LANES), F32),
                        pltpu.VMEM((past, LANES), F32)])
    return pl.pallas_call(
        functools.partial(_attn_sample_kernel, nb=nb, n_top=N_TOP),
        grid_spec=grid_spec,
        out_shape=(jax.ShapeDtypeStruct((nseq, t_new, ATT_W), F32),
                   jax.ShapeDtypeStruct((nseq, wb, 2 * KV_W), F32)),
        compiler_params=pltpu.CompilerParams(dimension_semantics=("arbitrary",),
                                             vmem_limit_bytes=SAMPLE_VMEM_LIMIT),
    )(page_table, q.reshape(nseq, t_new, ATT_W), gates.reshape(nseq, t_new, LANES),
      kvs.reshape(nseq, t_new, 2 * KV_W), kvw.reshape(nseq, t_new, 2 * KV_W), ckv0.reshape(nseq, 1, 2 * KV_W),
      cache_win.reshape(nseq, wb, 2 * KV_W), cache_cmp.reshape(n_pool, PAGE_SIZE, 2 * KV_W),
      cache_sel.reshape(n_pool, PAGE_SIZE, 2 * KV_W), wts["cp"], wts["wc"], cc, cs1, cs2,
      cmask, cand, forced, wmask, nmask, eexp)


def _sample_group(x_sample, cache_cmp, cache_sel, cache_win, state_pool, page_table, g_pre_mix, g_post_mix,
                  g_pre_ffn, g_post_ffn, wts):
    nseq, t_new, _ = x_sample.shape
    past = page_table.shape[1] * PAGE_SIZE
    wb = cache_win.shape[1]
    x2 = x_sample.reshape(nseq * t_new, D_MODEL)
    hist16 = jnp.pad(state_pool, ((0, 0), (HIST_ROWS - POOL_HIST, 0), (0, 0)))
    q, kvc, kvs, kvw, gates, pool_y, znew, ckv0 = _inproj_sample(x2, g_pre_mix, wts, hist16, nseq, t_new, past)
    o_att, win_new = _attn_sample(q, gates, kvs, kvw, ckv0, cache_cmp, cache_sel, cache_win, page_table, wts,
                                  t_new)
    y = _post(x2, o_att.reshape(nseq * t_new, ATT_W), pool_y, g_post_mix, g_pre_ffn, g_post_ffn, wts)
    kv6 = lambda a, r: a.reshape(1, nseq, r, 2, KV_HEADS, HEAD_DIM)
    return (y.reshape(nseq, t_new, D_MODEL), kv6(kvc, t_new), kv6(kvs, t_new), kv6(win_new, wb),
            znew[None, :, 1:, :])


def kernel(x_prompt, x_sample, cache_kv_cmp, cache_kv_sel, cache_kv_win, state_pool, page_table, g_pre_mix,
           g_post_mix, g_pre_ffn, g_post_ffn, w_in, cmp_pos, w_cmp, w_pool, pool_scale, w_out, w_gate, w_up, w_down):
    wts = _prep_weights(w_in[0], cmp_pos[0], w_cmp[0], w_pool[0], pool_scale[0], w_out[0], w_gate[0], w_up[0],
                        w_down[0])
    yp, cmp_p, sel_p, win_p, pool_p = _prompt_group(x_prompt, g_pre_mix[0], g_post_mix[0], g_pre_ffn[0],
                                                    g_post_ffn[0], wts)
    ys, cmp_s, sel_s, win_s, pool_s = _sample_group(x_sample, cache_kv_cmp[0], cache_kv_sel[0], cache_kv_win[0],
                                                    state_pool[0], page_table, g_pre_mix[0], g_post_mix[0],
                                                    g_pre_ffn[0], g_post_ffn[0], wts)
    return (yp, ys, cmp_p, sel_p, win_p, pool_p, cmp_s, sel_s, win_s, pool_s)
```

```python
import functools

import jax
import jax.numpy as jnp
import numpy as np
from jax import lax
from jax.experimental import pallas as pl
from jax.experimental.pallas import tpu as pltpu

F32 = jnp.float32
MXU_DTYPE = jnp.bfloat16

D_MODEL = 1024
PAGE_SIZE = 128
N_HEADS = 8
KV_HEADS = 2
HPG = N_HEADS // KV_HEADS
HEAD_DIM = 64
ATT_W = N_HEADS * HEAD_DIM
KV_W = KV_HEADS * HEAD_DIM
N_BRANCH = 3
L_CMP = 32
L_SEL = 64
N_SEL = 16
N_FORCED = 3
N_TOP = N_SEL - N_FORCED
WINDOW = 512
SCALE = HEAD_DIM ** -0.5
ROT_DIM = HEAD_DIM // 4
ROT_HALF = ROT_DIM // 2
ROPE_THETA = 500000.0
POOL_WINDOWS = (2, 4, 8, 16)
POOL_W = D_MODEL - ATT_W
POOL_GC = POOL_W // len(POOL_WINDOWS)
POOL_HIST = max(POOL_WINDOWS) - 1
HIST_ROWS = POOL_HIST + 1
N_GATE = N_HEADS * N_BRANCH
D_FF = -(-8 * D_MODEL // (3 * 256)) * 256
EPS = 1e-6
NEG = -1e30

LANES = 128
C_Q = 0
C_KVC = C_Q + ATT_W
C_KVS = C_KVC + 2 * KV_W
C_KVW = C_KVS + 2 * KV_W
C_POOL = C_KVW + 2 * KV_W
C_GATE = C_POOL + POOL_W
IN_W_PAD = C_GATE + LANES

TM_IN = 512
TM_POST = 512
Q_BLK = 128
K_CHUNK = 512
FF_CHUNKS = 2
VMEM_LIMIT = 56 * 1024 * 1024
SAMPLE_VMEM_LIMIT = 60 * 1024 * 1024


def _const_spec(shape):
    nd = len(shape)
    return pl.BlockSpec(shape, lambda *_: (0,) * nd, pipeline_mode=pl.Buffered(1))


def _rms(x, g):
    return x * lax.rsqrt(jnp.mean(x * x, axis=-1, keepdims=True) + EPS) * g


def _rope128(x, c, s1, s2):
    return x * c + pltpu.roll(x, ROT_HALF, 1) * s1 + pltpu.roll(x, LANES - ROT_HALF, 1) * s2


def _dot(a, b):
    return jnp.dot(a, b, preferred_element_type=F32)


def _dot_nt(a, b):
    return lax.dot_general(a, b, (((1,), (1,)), ((), ())), preferred_element_type=F32)


def _dot_tn(a, b):
    return lax.dot_general(a, b, (((0,), (0,)), ((), ())), preferred_element_type=F32)


def _rope_tables(pos):
    pos = jnp.asarray(np.asarray(pos), jnp.int32)
    inv = ROPE_THETA ** (-jnp.arange(ROT_HALF, dtype=F32) * 2.0 / ROT_DIM)
    ang = pos.astype(F32)[:, None] * inv
    cos, sin = jnp.cos(ang), jnp.sin(ang)
    n = pos.shape[0]
    one = jnp.ones((n, HEAD_DIM - ROT_DIM), F32)
    zero = jnp.zeros((n, HEAD_DIM - ROT_DIM), F32)
    zh = jnp.zeros((n, ROT_HALF), F32)
    c = jnp.concatenate([cos, cos, one], axis=1)
    s1 = jnp.concatenate([zh, sin, zero], axis=1)
    s2 = jnp.concatenate([-sin, zh, zero], axis=1)
    tile = lambda t: jnp.concatenate([t, t], axis=1)
    return tile(c), tile(s1), tile(s2)


def _prep_weights(w_in, cmp_pos, w_cmp, w_pool, pool_scale, w_out, w_gate, w_up, w_down):
    w_in_r = jnp.concatenate([
        w_in[:, :ATT_W + 6 * KV_W],
        w_in[:, ATT_W + 6 * KV_W + N_GATE:],
        w_in[:, ATT_W + 6 * KV_W:ATT_W + 6 * KV_W + N_GATE],
        jnp.zeros((D_MODEL, LANES - N_GATE), w_in.dtype)], axis=1).astype(MXU_DTYPE)
    cp = jnp.concatenate([cmp_pos[:, 0], cmp_pos[:, 0], cmp_pos[:, 1], cmp_pos[:, 1]], axis=1)
    wc = jnp.zeros((2 * KV_W, 2 * KV_W), F32)
    for i, c in enumerate((0, 0, 1, 1)):
        wc = wc.at[i * HEAD_DIM:(i + 1) * HEAD_DIM, i * HEAD_DIM:(i + 1) * HEAD_DIM].set(w_cmp[c])
    fc = D_FF // FF_CHUNKS
    return dict(
        w_in=w_in_r, cp=cp, wc=wc.astype(MXU_DTYPE), w_pool=w_pool.astype(MXU_DTYPE),
        pool_scale=pool_scale.reshape(1, POOL_W), w_out=w_out.astype(MXU_DTYPE),
        w_gate=w_gate.reshape(D_MODEL, FF_CHUNKS, fc).transpose(1, 0, 2).astype(MXU_DTYPE),
        w_up=w_up.reshape(D_MODEL, FF_CHUNKS, fc).transpose(1, 0, 2).astype(MXU_DTYPE),
        w_down=w_down.reshape(FF_CHUNKS, fc, D_MODEL).astype(MXU_DTYPE))


def _project_rows(x, g, w_ref, rc, rs1, rs2):
    h = _rms(x, g).astype(MXU_DTYPE)
    z = _dot(h, w_ref[...])
    q = jnp.concatenate(
        [_rope128(z[:, C_Q + i * LANES:C_Q + (i + 1) * LANES], rc, rs1, rs2) for i in range(ATT_W // LANES)],
        axis=1) * SCALE
    kvc = z[:, C_KVC:C_KVS]
    kvs = jnp.concatenate([_rope128(z[:, C_KVS:C_KVS + KV_W], rc, rs1, rs2), z[:, C_KVS + KV_W:C_KVW]], axis=1)
    kvw = jnp.concatenate([_rope128(z[:, C_KVW:C_KVW + KV_W], rc, rs1, rs2), z[:, C_KVW + KV_W:C_POOL]], axis=1)
    u = z[:, C_POOL:C_GATE]
    gates = z[:, C_GATE:IN_W_PAD]
    return q, kvc, kvs, kvw, u, gates


def _compress_rows(kvc, cp, wc_ref, cc, cs1, cs2):
    n = kvc.shape[0] // L_CMP
    summ = jnp.sum(kvc.reshape(n, L_CMP, 2 * KV_W) * cp[None], axis=1)
    ckv = _dot(summ.astype(MXU_DTYPE), wc_ref[...])
    return jnp.concatenate([_rope128(ckv[:, :KV_W], cc, cs1, cs2), ckv[:, KV_W:]], axis=1)


def _pool_windows(zf, rows, cnt):
    outs = []
    for gi, w in enumerate(POOL_WINDOWS):
        a = zf[:, gi * POOL_GC:(gi + 1) * POOL_GC]
        s, sh = a, 1
        while sh < w:
            s = s + pltpu.roll(s, sh, 0)
            sh *= 2
        outs.append(rows(s) / cnt[gi] - rows(a))
    return outs


def _pool_project(d_list, wp_ref, ps):
    y = jnp.concatenate([_dot(d.astype(MXU_DTYPE), wp_ref[gi]) for gi, d in enumerate(d_list)], axis=1)
    return y * ps


def _inproj_prompt_kernel(x_ref, g_ref, w_ref, rc_ref, rs1_ref, rs2_ref, cp_ref, wc_ref, cc_ref, cs1_ref, cs2_ref,
                          wp_ref, ps_ref,
                          q_ref, kvc_ref, kvs_ref, kvw_ref, kvsb_ref, kvwb_ref, gates_ref, pool_ref, ulast_ref,
                          ckv_ref, z_scr):
    j = pl.program_id(1)
    tm = x_ref.shape[0]
    nblk = tm // L_CMP
    r0 = pl.multiple_of(j * tm, tm)
    rows = pl.ds(r0, tm)
    q, kvc, kvs, kvw, u, gates = _project_rows(x_ref[...], g_ref[...], w_ref, rc_ref[rows, :], rs1_ref[rows, :],
                                               rs2_ref[rows, :])
    q_ref[...] = q.astype(q_ref.dtype)
    kvc_ref[...] = kvc
    kvs_ref[...] = kvs
    kvw_ref[...] = kvw
    kvsb_ref[...] = kvs.astype(kvsb_ref.dtype)
    kvwb_ref[...] = kvw.astype(kvwb_ref.dtype)
    gates_ref[...] = gates

    crow = pl.ds(pl.multiple_of(j * nblk, nblk), nblk)
    ckv_ref[...] = _compress_rows(kvc, cp_ref[...], wc_ref, cc_ref[crow, :], cs1_ref[crow, :], cs2_ref[crow, :])

    @pl.when(j == 0)
    def _():
        z_scr[0:HIST_ROWS, :] = jnp.zeros((HIST_ROWS, POOL_W), F32)

    @pl.when(j > 0)
    def _():
        z_scr[0:HIST_ROWS, :] = z_scr[tm:tm + HIST_ROWS, :]

    z_scr[HIST_ROWS:, :] = u
    pos = (r0 + lax.broadcasted_iota(jnp.int32, (tm, 1), 0)).astype(F32)
    cnt = [jnp.minimum(pos + 1.0, float(w)) for w in POOL_WINDOWS]
    d_list = _pool_windows(z_scr[...], lambda a: a[HIST_ROWS:, :], cnt)
    pool_ref[...] = _pool_project(d_list, wp_ref, ps_ref[...]).astype(pool_ref.dtype)
    ulast_ref[0] = u[tm - HIST_ROWS:, :]


def _inproj_prompt(x2, g_pre, wts, seq, batch):
    n = x2.shape[0]
    tm = min(TM_IN, seq)
    nj = seq // tm
    nblk = tm // L_CMP
    rc, rs1, rs2 = _rope_tables(np.arange(seq))
    cc, cs1, cs2 = _rope_tables((np.arange(seq // L_CMP) + 1) * L_CMP - 1)
    row = lambda w: pl.BlockSpec((tm, w), lambda b, j: (b * nj + j, 0))
    out_shape = (
        jax.ShapeDtypeStruct((n, ATT_W), MXU_DTYPE),
        jax.ShapeDtypeStruct((n, 2 * KV_W), F32),
        jax.ShapeDtypeStruct((n, 2 * KV_W), F32),
        jax.ShapeDtypeStruct((n, 2 * KV_W), F32),
        jax.ShapeDtypeStruct((n, 2 * KV_W), MXU_DTYPE),
        jax.ShapeDtypeStruct((n, 2 * KV_W), MXU_DTYPE),
        jax.ShapeDtypeStruct((n, LANES), F32),
        jax.ShapeDtypeStruct((n, POOL_W), MXU_DTYPE),
        jax.ShapeDtypeStruct((batch, HIST_ROWS, POOL_W), F32),
        jax.ShapeDtypeStruct((n // L_CMP, 2 * KV_W), F32),
    )
    out_specs = (row(ATT_W), row(2 * KV_W), row(2 * KV_W), row(2 * KV_W), row(2 * KV_W), row(2 * KV_W), row(LANES),
                 row(POOL_W), pl.BlockSpec((1, HIST_ROWS, POOL_W), lambda b, j: (b, 0, 0)),
                 pl.BlockSpec((nblk, 2 * KV_W), lambda b, j: (b * nj + j, 0)))
    in_specs = [row(D_MODEL), _const_spec((1, D_MODEL)), _const_spec((D_MODEL, IN_W_PAD)),
                _const_spec((seq, LANES)), _const_spec((seq, LANES)), _const_spec((seq, LANES)),
                _const_spec((L_CMP, 2 * KV_W)), _const_spec((2 * KV_W, 2 * KV_W)),
                _const_spec((seq // L_CMP, LANES)), _const_spec((seq // L_CMP, LANES)),
                _const_spec((seq // L_CMP, LANES)),
                _const_spec((len(POOL_WINDOWS), POOL_GC, POOL_GC)), _const_spec((1, POOL_W))]
    return pl.pallas_call(
        _inproj_prompt_kernel,
        grid=(batch, nj),
        in_specs=in_specs, out_specs=out_specs, out_shape=out_shape,
        scratch_shapes=[pltpu.VMEM((tm + HIST_ROWS, POOL_W), F32)],
        compiler_params=pltpu.CompilerParams(dimension_semantics=("arbitrary", "arbitrary"),
                                             vmem_limit_bytes=VMEM_LIMIT),
    )(x2, g_pre.reshape(1, D_MODEL), wts["w_in"], rc, rs1, rs2, wts["cp"], wts["wc"], cc, cs1, cs2,
      wts["w_pool"], wts["pool_scale"])


def _post_kernel(x_ref, o_ref, p_ref, wo_ref, gpm_ref, gpf_ref, gqf_ref, wg_ref, wu_ref, wd_ref, y_ref):
    mixed = _dot(jnp.concatenate([o_ref[...].astype(MXU_DTYPE), p_ref[...]], axis=1), wo_ref[...])
    x1 = x_ref[...] + _rms(mixed, gpm_ref[...])
    h2 = _rms(x1, gpf_ref[...]).astype(MXU_DTYPE)
    acc = None
    for c in range(FF_CHUNKS):
        f = (jax.nn.silu(_dot(h2, wg_ref[c])) * _dot(h2, wu_ref[c])).astype(MXU_DTYPE)
        part = _dot(f, wd_ref[c])
        acc = part if acc is None else acc + part
    y_ref[...] = x1 + _rms(acc, gqf_ref[...])


def _post(x2, o_att, pool_y, g_post_mix, g_pre_ffn, g_post_ffn, wts):
    n = x2.shape[0]
    tm = min(TM_POST, n)
    fc = D_FF // FF_CHUNKS
    row = lambda w: pl.BlockSpec((tm, w), lambda i: (i, 0))
    return pl.pallas_call(
        _post_kernel,
        grid=(n // tm,),
        in_specs=[row(D_MODEL), row(ATT_W), row(POOL_W), _const_spec((D_MODEL, D_MODEL)),
                  _const_spec((1, D_MODEL)), _const_spec((1, D_MODEL)), _const_spec((1, D_MODEL)),
                  _const_spec((FF_CHUNKS, D_MODEL, fc)), _const_spec((FF_CHUNKS, D_MODEL, fc)),
                  _const_spec((FF_CHUNKS, fc, D_MODEL))],
        out_specs=row(D_MODEL),
        out_shape=jax.ShapeDtypeStruct((n, D_MODEL), F32),
        compiler_params=pltpu.CompilerParams(dimension_semantics=("arbitrary",), vmem_limit_bytes=VMEM_LIMIT),
    )(x2, o_att, pool_y, wts["w_out"], g_post_mix.reshape(1, D_MODEL), g_pre_ffn.reshape(1, D_MODEL),
      g_post_ffn.reshape(1, D_MODEL), wts["w_gate"], wts["w_up"], wts["w_down"])


def _select_blocks_t(p_slc_t, c_row, n_top):
    nb = p_slc_t.shape[0]
    jj = lax.broadcasted_iota(jnp.int32, p_slc_t.shape, 0)
    cand = (jj >= 1) & (jj <= c_row - 2)
    sc = jnp.where(cand, p_slc_t, -1.0)
    rank = jnp.zeros(sc.shape, F32)
    for i in range(nb):
        row = sc[i:i + 1, :]
        beats = (row > sc) | ((row == sc) & (jj > i))
        rank = rank + jnp.where(beats, 1.0, 0.0)
    return (cand & (rank < n_top)) | (jj == 0) | (jj == c_row) | (jj == c_row - 1)


def _lane_tiles(a, n):
    return jnp.concatenate([a] * n, axis=1)


def _attn_prompt_kernel(q_ref, gt_ref, ckv_ref, ks_ref, kw_ref, e_ref, o_ref,
                        s_scr, mel_scr, mb_scr, lel_scr, acc_scr):
    qb = pl.program_id(1)
    qblk = q_ref.shape[0]
    rows = HPG * qblk
    ncb = ckv_ref.shape[1]
    half = ncb // 2
    wspan = WINDOW + qblk
    start = qb * qblk
    tpos_c = start + lax.broadcasted_iota(jnp.int32, (qblk, 1), 0)
    tpos_r = start + lax.broadcasted_iota(jnp.int32, (1, qblk), 1)
    gs = jax.nn.sigmoid(gt_ref[...])
    ckv = ckv_ref[0]
    ck = ckv[:, :KV_W].astype(MXU_DTYPE)
    cv = ckv[:, KV_W:].astype(MXU_DTYPE)
    crow = lax.broadcasted_iota(jnp.int32, (ncb, rows), 0)
    cblk = 2 * (crow % half) + crow // half
    cmask = ((cblk + 1) * L_CMP - 1) <= _lane_tiles(tpos_r, HPG)

    ws = pl.multiple_of(jnp.maximum(start - WINDOW, 0), qblk)
    dwin = tpos_c - (ws + lax.broadcasted_iota(jnp.int32, (1, wspan), 1))
    bias_w = jnp.where((dwin >= 0) & (dwin < WINDOW), 0.0, NEG)
    n_chunks = start // K_CHUNK + 1
    lane = lax.broadcasted_iota(jnp.int32, (qblk, LANES), 1)

    outs = [None] * N_HEADS
    for g in range(KV_HEADS):
        pieces = []
        for h in range(HPG):
            hh = g * HPG + h
            tile = q_ref[:, (hh // 2) * LANES:(hh // 2 + 1) * LANES].astype(F32)
            if hh % 2 != g:
                tile = pltpu.roll(tile, HEAD_DIM, 1)
            pieces.append(jnp.where((lane >= g * HEAD_DIM) & (lane < (g + 1) * HEAD_DIM), tile, 0.0))
        qg = jnp.concatenate(pieces, axis=0).astype(MXU_DTYPE)

        s = jnp.where(cmask, _dot_nt(ck, qg), NEG)
        e = jnp.where(cmask, jnp.exp(s - jnp.max(s, axis=0, keepdims=True)), 0.0)
        p = e / jnp.maximum(jnp.sum(e, axis=0, keepdims=True), 1e-30)
        o_c = _dot_tn(p.astype(MXU_DTYPE), cv)
        pc = p[:, 0:qblk]
        for h in range(1, HPG):
            pc = pc + p[:, h * qblk:(h + 1) * qblk]
        sel = _select_blocks_t(pc[0:half] + pc[half:ncb], tpos_r // L_SEL, N_TOP)
        selm = jnp.where(sel, 1.0, 0.0).astype(MXU_DTYPE)

        mel_scr[...] = jnp.full((rows, LANES), NEG, F32)

        def score_body(ci, _):
            k0 = pl.multiple_of(ci * K_CHUNK, K_CHUNK)
            sv = _dot_nt(qg, ks_ref[0, pl.ds(k0, K_CHUNK), 0:KV_W])
            kpos = k0 + lax.broadcasted_iota(jnp.int32, (1, K_CHUNK), 1)
            keep = (_dot_tn(selm, e_ref[ci]) > 0.5) & (kpos <= tpos_c)
            bias = jnp.where(keep, 0.0, NEG)
            for h in range(HPG):
                rs = slice(h * qblk, (h + 1) * qblk)
                sh = sv[rs] + bias
                s_scr[ci, rs, :] = sh
                mh = sh[:, 0:LANES]
                for j in range(1, K_CHUNK // LANES):
                    mh = jnp.maximum(mh, sh[:, j * LANES:(j + 1) * LANES])
                mel_scr[rs, :] = jnp.maximum(mel_scr[rs, :], mh)
            return 0

        lax.fori_loop(0, n_chunks, score_body, 0)
        mb_scr[...] = jnp.broadcast_to(jnp.max(mel_scr[...], axis=1, keepdims=True), (rows, LANES))
        lel_scr[...] = jnp.zeros((rows, LANES), F32)
        acc_scr[...] = jnp.zeros((rows, LANES), F32)

        def value_body(ci, _):
            k0 = pl.multiple_of(ci * K_CHUNK, K_CHUNK)
            pv = jnp.exp(s_scr[ci] - _lane_tiles(mb_scr[...], K_CHUNK // LANES))
            lsum = pv[:, 0:LANES]
            for j in range(1, K_CHUNK // LANES):
                lsum = lsum + pv[:, j * LANES:(j + 1) * LANES]
            lel_scr[...] += lsum
            acc_scr[...] += _dot(pv.astype(MXU_DTYPE), ks_ref[0, pl.ds(k0, K_CHUNK), KV_W:2 * KV_W])
            return 0

        lax.fori_loop(0, n_chunks, value_body, 0)
        o_s = acc_scr[...] / jnp.maximum(jnp.sum(lel_scr[...], axis=1, keepdims=True), 1e-30)

        sw = _dot_nt(qg, kw_ref[0, pl.ds(ws, wspan), 0:KV_W])
        sw = jnp.concatenate([sw[h * qblk:(h + 1) * qblk] + bias_w for h in range(HPG)], axis=0)
        ew = jnp.exp(sw - jnp.max(sw, axis=1, keepdims=True))
        o_w = (_dot(ew.astype(MXU_DTYPE), kw_ref[0, pl.ds(ws, wspan), KV_W:2 * KV_W])
               / jnp.maximum(jnp.sum(ew, axis=1, keepdims=True), 1e-30))

        for h in range(HPG):
            hh = g * HPG + h
            col = hh * N_BRANCH
            rs = slice(h * qblk, (h + 1) * qblk)
            cs = slice(g * HEAD_DIM, (g + 1) * HEAD_DIM)
            outs[hh] = (gs[:, col:col + 1] * o_c[rs, cs] + gs[:, col + 1:col + 2] * o_s[rs, cs]
                        + gs[:, col + 2:col + 3] * o_w[rs, cs])
    o_ref[...] = jnp.concatenate(outs, axis=1).astype(o_ref.dtype)


def _attn_prompt(q, gates, ckv_perm, kvsb, kvwb, seq, batch):
    n = q.shape[0]
    qblk = min(Q_BLK, seq)
    nqb = seq // qblk
    nb = seq // L_SEL
    nchunk = seq // K_CHUNK
    rows = HPG * qblk
    assert seq >= WINDOW + qblk and seq % K_CHUNK == 0 and K_CHUNK % qblk == 0
    e = (np.arange(seq)[None, :] // L_SEL == np.arange(nb)[:, None]).astype(np.float32)
    e = jnp.asarray(e.reshape(nb, nchunk, K_CHUNK).transpose(1, 0, 2), MXU_DTYPE)
    row = lambda w: pl.BlockSpec((qblk, w), lambda b, i: (b * nqb + i, 0))
    per_b = lambda r, w: pl.BlockSpec((1, r, w), lambda b, i: (b, 0, 0))
    return pl.pallas_call(
        _attn_prompt_kernel,
        grid=(batch, nqb),
        in_specs=[row(ATT_W), row(LANES), per_b(seq // L_CMP, 2 * KV_W), per_b(seq, 2 * KV_W),
                  per_b(seq, 2 * KV_W), _const_spec((nchunk, nb, K_CHUNK))],
        out_specs=row(ATT_W),
        out_shape=jax.ShapeDtypeStruct((n, ATT_W), MXU_DTYPE),
        scratch_shapes=[pltpu.VMEM((nchunk, rows, K_CHUNK), F32), pltpu.VMEM((rows, LANES), F32),
                        pltpu.VMEM((rows, LANES), F32), pltpu.VMEM((rows, LANES), F32),
                        pltpu.VMEM((rows, LANES), F32)],
        compiler_params=pltpu.CompilerParams(dimension_semantics=("arbitrary", "arbitrary"),
                                             vmem_limit_bytes=VMEM_LIMIT),
    )(q, gates, ckv_perm, kvsb.reshape(batch, seq, 2 * KV_W), kvwb.reshape(batch, seq, 2 * KV_W), e)


def _prompt_group(x_prompt, g_pre_mix, g_post_mix, g_pre_ffn, g_post_ffn, wts):
    batch, seq, _ = x_prompt.shape
    x2 = x_prompt.reshape(batch * seq, D_MODEL)
    q, kvc, kvs, kvw, kvsb, kvwb, gates, pool_y, ulast, ckv = _inproj_prompt(x2, g_pre_mix, wts, seq, batch)
    ncb = seq // L_CMP
    ckv_perm = ckv.reshape(batch, ncb // 2, 2, 2 * KV_W).transpose(0, 2, 1, 3).reshape(batch, ncb, 2 * KV_W)
    o_att = _attn_prompt(q, gates, ckv_perm, kvsb, kvwb, seq, batch)
    y = _post(x2, o_att, pool_y, g_post_mix, g_pre_ffn, g_post_ffn, wts)
    kv6 = lambda a: a.reshape(1, batch, seq, 2, KV_HEADS, HEAD_DIM)
    wkeep = min(WINDOW, seq)
    return (y.reshape(batch, seq, D_MODEL), kv6(kvc), kv6(kvs), kv6(kvw)[:, :, seq - wkeep:],
            ulast[None, :, 1:, :])


def _inproj_sample_kernel(x_ref, g_ref, w_ref, rc_ref, rs1_ref, rs2_ref, cp_ref, wc_ref, cc_ref, cs1_ref, cs2_ref,
                          wp_ref, ps_ref, hist_ref,
                          q_ref, kvc_ref, kvs_ref, kvw_ref, gates_ref, pool_ref, znew_ref, ckv_ref, z_scr, *, past):
    nseq, zrows, _ = z_scr.shape
    t_new = zrows - HIST_ROWS
    n = nseq * t_new
    q, kvc, kvs, kvw, u, gates = _project_rows(x_ref[...], g_ref[...], w_ref, rc_ref[...], rs1_ref[...],
                                               rs2_ref[...])
    q_ref[...] = q
    kvc_ref[...] = kvc
    kvs_ref[...] = kvs
    kvw_ref[...] = kvw
    gates_ref[...] = gates

    summ = jnp.sum(kvc.reshape(nseq, t_new, 2 * KV_W) * cp_ref[0:t_new, :][None], axis=1)
    ckv = _dot(summ.astype(MXU_DTYPE), wc_ref[...])
    ckv_ref[...] = jnp.concatenate([_rope128(ckv[:, :KV_W], cc_ref[...], cs1_ref[...], cs2_ref[...]),
                                    ckv[:, KV_W:]], axis=1)

    z_scr[:, 0:HIST_ROWS, :] = hist_ref[...]
    z_scr[:, HIST_ROWS:, :] = u.reshape(nseq, t_new, POOL_W)
    zf = z_scr[...].reshape(nseq * zrows, POOL_W)
    tok = lax.broadcasted_iota(jnp.int32, (n, 1), 0) % t_new
    pos = (past + tok).astype(F32)
    cnt = [jnp.minimum(pos + 1.0, float(w)) for w in POOL_WINDOWS]
    take = lambda a: a.reshape(nseq, zrows, POOL_GC)[:, HIST_ROWS:, :].reshape(n, POOL_GC)
    d_list = _pool_windows(zf, take, cnt)
    pool_ref[...] = _pool_project(d_list, wp_ref, ps_ref[...]).astype(pool_ref.dtype)
    znew_ref[...] = z_scr[:, zrows - HIST_ROWS:, :]


def _inproj_sample(x2, g_pre, wts, hist16, nseq, t_new, past):
    n = x2.shape[0]
    rc, rs1, rs2 = _rope_tables(np.tile(past + np.arange(t_new), nseq))
    cc, cs1, cs2 = _rope_tables(np.array([past + L_CMP - 1]))
    full = lambda *s: pl.BlockSpec(s, lambda i: (0,) * len(s))
    out_shape = (
        jax.ShapeDtypeStruct((n, ATT_W), F32),
        jax.ShapeDtypeStruct((n, 2 * KV_W), F32),
        jax.ShapeDtypeStruct((n, 2 * KV_W), F32),
        jax.ShapeDtypeStruct((n, 2 * KV_W), F32),
        jax.ShapeDtypeStruct((n, LANES), F32),
        jax.ShapeDtypeStruct((n, POOL_W), MXU_DTYPE),
        jax.ShapeDtypeStruct((nseq, HIST_ROWS, POOL_W), F32),
        jax.ShapeDtypeStruct((nseq, 2 * KV_W), F32),
    )
    out_specs = (full(n, ATT_W), full(n, 2 * KV_W), full(n, 2 * KV_W), full(n, 2 * KV_W), full(n, LANES),
                 full(n, POOL_W), full(nseq, HIST_ROWS, POOL_W), full(nseq, 2 * KV_W))
    in_specs = [full(n, D_MODEL), full(1, D_MODEL), full(D_MODEL, IN_W_PAD), full(n, LANES), full(n, LANES),
                full(n, LANES), full(L_CMP, 2 * KV_W), full(2 * KV_W, 2 * KV_W), full(1, LANES), full(1, LANES),
                full(1, LANES), full(len(POOL_WINDOWS), POOL_GC, POOL_GC), full(1, POOL_W),
                full(nseq, HIST_ROWS, POOL_W)]
    return pl.pallas_call(
        functools.partial(_inproj_sample_kernel, past=past),
        grid=(1,),
        in_specs=in_specs, out_specs=out_specs, out_shape=out_shape,
        scratch_shapes=[pltpu.VMEM((nseq, HIST_ROWS + t_new, POOL_W), F32)],
        compiler_params=pltpu.CompilerParams(dimension_semantics=("arbitrary",), vmem_limit_bytes=VMEM_LIMIT),
    )(x2, g_pre.reshape(1, D_MODEL), wts["w_in"], rc, rs1, rs2, wts["cp"], wts["wc"], cc, cs1, cs2,
      wts["w_pool"], wts["pool_scale"], hist16)


PAGE_CHUNK = 8
CMP_LANE_BLKS = PAGE_SIZE // L_CMP
HALF_PAGES = LANES // CMP_LANE_BLKS
CMP_PAGES_PER_STEP = 2


def _sample_tables(past, t_new, wb):
    npages = past // PAGE_SIZE
    nrow = HPG * KV_HEADS * t_new
    r = np.arange(nrow)
    t_row = r % t_new
    pos = past + t_row
    lanes = np.arange(npages * CMP_LANE_BLKS)
    page = HALF_PAGES * (lanes // LANES) + lanes % HALF_PAGES
    m = (lanes % LANES) // HALF_PAGES
    blk = CMP_LANE_BLKS * page + (m - 1) % CMP_LANE_BLKS
    cmask = (blk[None, :] + 1) * L_CMP - 1 <= pos[:, None]
    n_new = 2 * t_new
    blk_new = past // L_CMP + np.arange(n_new)
    n_cmp = (past + -(-t_new // L_SEL) * L_SEL) // L_CMP
    cmask_new = ((blk_new[None, :] + 1) * L_CMP - 1 <= pos[:, None]) & (blk_new[None, :] < n_cmp)
    jidx = np.where(m == 1, 2 * page, np.where(m == 3, 2 * page + 1, -1))
    rg = np.arange(KV_HEADS * t_new)
    c = ((past + rg % t_new) // L_SEL)[:, None]
    jj = jidx[None, :]
    cand = (jj >= 1) & (jj <= c - 2)
    forced = (jj >= 0) & ((jj == 0) | (jj == c) | (jj == c - 1))
    l = np.arange(n_new)
    new_blk = past // L_SEL
    c_row = (pos // L_SEL)[:, None]
    new_forced = (new_blk == 0) | (new_blk == c_row) | (new_blk == c_row - 1)
    nmask = new_forced & (l[None, :] <= t_row[:, None]) & (l[None, :] < t_new)
    i = np.arange(wb)
    d = t_row[:, None] + wb - i[None, :]
    wmask = (d >= 0) & (d < WINDOW) & (past - wb + i[None, :] >= 0)
    dn = t_row[:, None] - l[None, :]
    wmask_new = (dn >= 0) & (dn < WINDOW) & (l[None, :] < t_new)
    ck = PAGE_CHUNK * PAGE_SIZE
    k = np.arange(ck)
    eeo = np.zeros((npages // PAGE_CHUNK, 2 * npages, ck), np.float32)
    for ch in range(npages // PAGE_CHUNK):
        pg = PAGE_CHUNK * ch + k // PAGE_SIZE
        upper = (k % PAGE_SIZE) >= L_SEL
        eeo[ch, pg[~upper], k[~upper]] = 1.0
        eeo[ch, npages + pg[upper], k[upper]] = 1.0
    tt = np.arange(PAGE_SIZE)
    steps = HALF_PAGES // CMP_PAGES_PER_STEP
    gsum = np.zeros((steps, 2 * CMP_PAGES_PER_STEP * PAGE_SIZE, LANES), np.float32)
    for it in range(steps):
        for rep in range(2):
            for kk in range(CMP_PAGES_PER_STEP):
                r0 = (rep * CMP_PAGES_PER_STEP + kk) * PAGE_SIZE
                col = HALF_PAGES * ((tt // L_CMP + 1) % CMP_LANE_BLKS) + it * CMP_PAGES_PER_STEP + kk
                gsum[it, r0 + tt, col] = 1.0
    f = lambda a: jnp.asarray(np.asarray(a, np.float32))
    return dict(gsum=jnp.asarray(gsum, MXU_DTYPE), cmask=f(cmask), cmask_new=f(cmask_new), cand=f(cand), forced=f(forced), nmask=f(nmask),
                wmask=f(wmask), wmask_new=f(wmask_new), jidx=jnp.asarray(jidx[None, :], jnp.int32),
                eeo=jnp.asarray(eeo, MXU_DTYPE), blk=blk, jidx_np=jidx)


def _rope_tables_t(pos):
    c, s1, s2 = _rope_tables(pos)
    return c.T, s1.T, s2.T


def _attn_sample_kernel(pt_ref, q_ref, gt_ref, kvs_ref, kvw_ref, ckv0_ref, cw_ref, cmp_hbm, sel_hbm,
                        cpt_ref, wct_ref, cc_ref, cs1_ref, cs2_ref, cmask_ref, cmaskn_ref, cand_ref, forced_ref,
                        nmask_ref, wmask_ref, wmaskn_ref, jidx_ref, eeo_ref, gsum_ref,
                        o_ref,
                        cbuf, sbuf, sem, summ_scr, s_scr, *, jidx_np, n_top):
    b = pl.program_id(0)
    nseq = pl.num_programs(0)
    npages = pt_ref.shape[1]
    t_new = q_ref.shape[1]
    nrow = HPG * KV_HEADS * t_new
    ngt = KV_HEADS * t_new
    slot = b % 2

    def copies(seq, sl):
        out = []
        for p in range(npages):
            pg = pt_ref[seq, p]
            out.append(pltpu.make_async_copy(cmp_hbm.at[pg], cbuf.at[sl, p], sem.at[0, sl]))
            out.append(pltpu.make_async_copy(sel_hbm.at[pg], sbuf.at[sl, p], sem.at[1, sl]))
        return out

    @pl.when(b == 0)
    def _():
        for cpy in copies(b, slot):
            cpy.start()

    @pl.when(b + 1 < nseq)
    def _():
        for cpy in copies(b + 1, 1 - slot):
            cpy.start()

    qb = q_ref[0]
    zeros = jnp.zeros((t_new, HEAD_DIM), F32)
    blocks = []
    for h in range(HPG):
        for g in range(KV_HEADS):
            piece = qb[:, (g * HPG + h) * HEAD_DIM:(g * HPG + h + 1) * HEAD_DIM]
            blocks.append(jnp.concatenate([piece, zeros] if g == 0 else [zeros, piece], axis=1))
    qbd = jnp.concatenate(blocks, axis=0).astype(MXU_DTYPE)

    def new_rows(rows):
        a = jnp.concatenate([rows, jnp.zeros((2 * t_new - rows.shape[0], 2 * KV_W), F32)], axis=0)
        return a[:, :KV_W].astype(MXU_DTYPE), a[:, KV_W:].astype(MXU_DTYPE)

    def joint_softmax(s1, ok1, s2, ok2):
        s1 = jnp.where(ok1, s1, NEG)
        s2 = jnp.where(ok2, s2, NEG)
        m = jnp.maximum(jnp.max(s1, axis=1, keepdims=True), jnp.max(s2, axis=1, keepdims=True))
        e1 = jnp.where(ok1, jnp.exp(s1 - m), 0.0)
        e2 = jnp.where(ok2, jnp.exp(s2 - m), 0.0)
        l = jnp.maximum(jnp.sum(e1, axis=1, keepdims=True) + jnp.sum(e2, axis=1, keepdims=True), 1e-30)
        return e1 / l, e2 / l

    kwn, vwn = new_rows(kvw_ref[0])
    p1, p2 = joint_softmax(_dot(qbd, cw_ref[0, 0:KV_W, :].astype(MXU_DTYPE)), wmask_ref[...] > 0.5,
                           _dot_nt(qbd, kwn), wmaskn_ref[...] > 0.5)
    o_w = (_dot_nt(p1.astype(MXU_DTYPE), cw_ref[0, KV_W:2 * KV_W, :].astype(MXU_DTYPE))
           + _dot(p2.astype(MXU_DTYPE), vwn))

    for cpy in copies(b, slot):
        cpy.wait()

    nhalf = npages // HALF_PAGES
    summ_scr[...] = jnp.zeros(summ_scr.shape, F32)

    def cmp_body(it, _):
        his, los = [], []
        for k in range(CMP_PAGES_PER_STEP):
            pg = it * CMP_PAGES_PER_STEP + k
            x = jnp.concatenate([cbuf[slot, h2 * HALF_PAGES + pg] * cpt_ref[...] for h2 in range(nhalf)], axis=0)
            hi = x.astype(MXU_DTYPE)
            his.append(hi)
            los.append((x - hi.astype(F32)).astype(MXU_DTYPE))
        summ_scr[...] += _dot(jnp.concatenate(his + los, axis=1), gsum_ref[it])
        return 0

    lax.fori_loop(0, HALF_PAGES // CMP_PAGES_PER_STEP, cmp_body, 0)
    summ = jnp.concatenate([summ_scr[h2 * 2 * KV_W:(h2 + 1) * 2 * KV_W] for h2 in range(nhalf)], axis=1)
    ckv = _dot(wct_ref[...], summ.astype(MXU_DTYPE))
    kc = ckv[0:KV_W]
    ckt = (kc * cc_ref[...] + pltpu.roll(kc, ROT_HALF, 0) * cs1_ref[...]
           + pltpu.roll(kc, KV_W - ROT_HALF, 0) * cs2_ref[...]).astype(MXU_DTYPE)
    cvt = ckv[KV_W:2 * KV_W].astype(MXU_DTYPE)

    ckn, cvn = new_rows(ckv0_ref[0])
    p1, p2 = joint_softmax(_dot(qbd, ckt), cmask_ref[...] > 0.5, _dot_nt(qbd, ckn), cmaskn_ref[...] > 0.5)
    o_c = _dot_nt(p1.astype(MXU_DTYPE), cvt) + _dot(p2.astype(MXU_DTYPE), cvn)
    pc = p1[0:ngt]
    for h in range(1, HPG):
        pc = pc + p1[h * ngt:(h + 1) * ngt]
    pair = jnp.concatenate(
        [pc[:, t * LANES:(t + 1) * LANES] + pltpu.roll(pc[:, t * LANES:(t + 1) * LANES], LANES - HALF_PAGES, 1)
         for t in range(pc.shape[1] // LANES)], axis=1)

    cand = cand_ref[...] > 0.5
    sc = jnp.where(cand, pair, -1.0)
    jl = jidx_ref[...]
    rank = jnp.zeros(sc.shape, F32)
    for i in np.nonzero(jidx_np >= 0)[0]:
        col = sc[:, int(i):int(i) + 1]
        beats = (col > sc) | ((col == sc) & (jl > int(jidx_np[i])))
        rank = rank + jnp.where(beats, 1.0, 0.0)
    sel = jnp.where((cand & (rank < n_top)) | (forced_ref[...] > 0.5), 1.0, 0.0)
    sel_eo = jnp.concatenate(
        [sel[:, t * LANES + HALF_PAGES:t * LANES + 2 * HALF_PAGES] for t in range(sel.shape[1] // LANES)]
        + [sel[:, t * LANES + 3 * HALF_PAGES:(t + 1) * LANES] for t in range(sel.shape[1] // LANES)],
        axis=1).astype(MXU_DTYPE)

    def score_body(ch, mel):
        keep = _dot(sel_eo, eeo_ref[ch]) > 0.5
        bias = jnp.where(keep, 0.0, NEG)
        bias = jnp.concatenate([bias] * HPG, axis=0)
        for pp in range(PAGE_CHUNK):
            pg = ch * PAGE_CHUNK + pp
            sv = _dot(qbd, sbuf[slot, pg, 0:KV_W, :].astype(MXU_DTYPE)) + bias[:, pp * LANES:(pp + 1) * LANES]
            s_scr[pg] = sv
            mel = jnp.maximum(mel, sv)
        return mel

    mel = lax.fori_loop(0, npages // PAGE_CHUNK, score_body, jnp.full((nrow, LANES), NEG, F32))
    ksn, vsn = new_rows(kvs_ref[0])
    nok = nmask_ref[...] > 0.5
    s_new = jnp.where(nok, _dot_nt(qbd, ksn), NEG)
    m = jnp.maximum(jnp.max(mel, axis=1, keepdims=True), jnp.max(s_new, axis=1, keepdims=True))
    mb = jnp.broadcast_to(m, (nrow, LANES))

    def value_body(ch, carry):
        lel, acc = carry
        for pp in range(PAGE_CHUNK):
            pg = ch * PAGE_CHUNK + pp
            pv = jnp.exp(s_scr[pg] - mb)
            lel = lel + pv
            acc = acc + _dot_nt(pv.astype(MXU_DTYPE), sbuf[slot, pg, KV_W:2 * KV_W, :].astype(MXU_DTYPE))
        return lel, acc

    e_new = jnp.where(nok, jnp.exp(s_new - m), 0.0)
    lel, acc = lax.fori_loop(0, npages // PAGE_CHUNK, value_body,
                             (jnp.zeros((nrow, LANES), F32), _dot(e_new.astype(MXU_DTYPE), vsn)))
    l = jnp.sum(lel, axis=1, keepdims=True) + jnp.sum(e_new, axis=1, keepdims=True)
    o_s = acc / jnp.maximum(l, 1e-30)

    gs = jax.nn.sigmoid(gt_ref[0])
    outs = [None] * N_HEADS
    for h in range(HPG):
        for g in range(KV_HEADS):
            hh = g * HPG + h
            rs = slice((h * KV_HEADS + g) * t_new, (h * KV_HEADS + g + 1) * t_new)
            cs = slice(g * HEAD_DIM, (g + 1) * HEAD_DIM)
            col = hh * N_BRANCH
            outs[hh] = (gs[:, col:col + 1] * o_c[rs, cs] + gs[:, col + 1:col + 2] * o_s[rs, cs]
                        + gs[:, col + 2:col + 3] * o_w[rs, cs])
    o_ref[0] = jnp.concatenate(outs, axis=1)


def _dims_major(a):
    n, toks = a.shape[:2]
    return jnp.transpose(a, (0, 2, 3, 4, 1)).reshape(n, 2 * KV_W, toks)


def _attn_sample(q, gates, kvs, kvw, ckv0, cache_cmp, cache_sel, cache_win, page_table, wts, t_new):
    nseq, npages = page_table.shape
    past = npages * PAGE_SIZE
    wb = cache_win.shape[1]
    nrow = HPG * KV_HEADS * t_new
    ngt = KV_HEADS * t_new
    nlane = npages * CMP_LANE_BLKS
    assert t_new <= L_CMP and past % L_SEL == 0 and wb + t_new >= WINDOW
    assert npages % HALF_PAGES == 0 and npages % PAGE_CHUNK == 0 and 2 * npages == LANES
    tb = _sample_tables(past, t_new, wb)
    cc, cs1, cs2 = _rope_tables_t((tb["blk"] + 1) * L_CMP - 1)
    cpt = jnp.tile(wts["cp"].T, (1, CMP_LANE_BLKS))
    seq3 = lambda r, w: pl.BlockSpec((1, r, w), lambda b, pt: (b, 0, 0))
    const = lambda *s: pl.BlockSpec(s, lambda b, pt: (0,) * len(s), pipeline_mode=pl.Buffered(1))
    hbm = pl.BlockSpec(memory_space=pl.ANY)
    in_specs = [seq3(t_new, ATT_W), seq3(t_new, LANES), seq3(t_new, 2 * KV_W), seq3(t_new, 2 * KV_W),
                seq3(1, 2 * KV_W), seq3(2 * KV_W, wb), hbm, hbm,
                const(2 * KV_W, PAGE_SIZE), const(2 * KV_W, 2 * KV_W), const(KV_W, nlane), const(KV_W, nlane),
                const(KV_W, nlane), const(nrow, nlane), const(nrow, 2 * t_new), const(ngt, nlane),
                const(ngt, nlane), const(nrow, 2 * t_new), const(nrow, wb), const(nrow, 2 * t_new),
                const(1, nlane), const(npages // PAGE_CHUNK, 2 * npages, PAGE_CHUNK * PAGE_SIZE),
                const(HALF_PAGES // CMP_PAGES_PER_STEP, 2 * CMP_PAGES_PER_STEP * PAGE_SIZE, LANES)]
    grid_spec = pltpu.PrefetchScalarGridSpec(
        num_scalar_prefetch=1, grid=(nseq,), in_specs=in_specs,
        out_specs=seq3(t_new, ATT_W),
        scratch_shapes=[pltpu.VMEM((2, npages, 2 * KV_W, PAGE_SIZE), F32),
                        pltpu.VMEM((2, npages, 2 * KV_W, PAGE_SIZE), F32),
                        pltpu.SemaphoreType.DMA((2, 2)),
                        pltpu.VMEM((nlane // LANES * 2 * KV_W, LANES), F32),
                        pltpu.VMEM((npages, nrow, PAGE_SIZE), F32)])
    return pl.pallas_call(
        functools.partial(_attn_sample_kernel, jidx_np=tb["jidx_np"], n_top=N_TOP),
        grid_spec=grid_spec,
        out_shape=jax.ShapeDtypeStruct((nseq, t_new, ATT_W), F32),
        compiler_params=pltpu.CompilerParams(dimension_semantics=("arbitrary",),
                                             vmem_limit_bytes=SAMPLE_VMEM_LIMIT),
    )(page_table, q.reshape(nseq, t_new, ATT_W), gates.reshape(nseq, t_new, LANES),
      kvs.reshape(nseq, t_new, 2 * KV_W), kvw.reshape(nseq, t_new, 2 * KV_W), ckv0.reshape(nseq, 1, 2 * KV_W),
      _dims_major(cache_win), _dims_major(cache_cmp), _dims_major(cache_sel), cpt, wts["wc"].T, cc, cs1, cs2,
      tb["cmask"], tb["cmask_new"], tb["cand"], tb["forced"], tb["nmask"], tb["wmask"], tb["wmask_new"],
      tb["jidx"], tb["eeo"], tb["gsum"])


def _sample_group(x_sample, cache_cmp, cache_sel, cache_win, state_pool, page_table, g_pre_mix, g_post_mix,
                  g_pre_ffn, g_post_ffn, wts):
    nseq, t_new, _ = x_sample.shape
    past = page_table.shape[1] * PAGE_SIZE
    wb = cache_win.shape[1]
    x2 = x_sample.reshape(nseq * t_new, D_MODEL)
    hist16 = jnp.pad(state_pool, ((0, 0), (HIST_ROWS - POOL_HIST, 0), (0, 0)))
    q, kvc, kvs, kvw, gates, pool_y, znew, ckv0 = _inproj_sample(x2, g_pre_mix, wts, hist16, nseq, t_new, past)
    o_att = _attn_sample(q, gates, kvs, kvw, ckv0, cache_cmp, cache_sel, cache_win, page_table, wts, t_new)
    y = _post(x2, o_att.reshape(nseq * t_new, ATT_W), pool_y, g_post_mix, g_pre_ffn, g_post_ffn, wts)
    kv6 = lambda a: a.reshape(1, nseq, t_new, 2, KV_HEADS, HEAD_DIM)
    wkeep = min(WINDOW, wb + t_new)
    win_new = jnp.concatenate([cache_win[:, wb + t_new - wkeep:], kv6(kvw)[0]], axis=1)[None]
    return (y.reshape(nseq, t_new, D_MODEL), kv6(kvc), kv6(kvs), win_new, znew[None, :, 1:, :])


def kernel(x_prompt, x_sample, cache_kv_cmp, cache_kv_sel, cache_kv_win, state_pool, page_table, g_pre_mix,
           g_post_mix, g_pre_ffn, g_post_ffn, w_in, cmp_pos, w_cmp, w_pool, pool_scale, w_out, w_gate, w_up, w_down):
    wts = _prep_weights(w_in[0], cmp_pos[0], w_cmp[0], w_pool[0], pool_scale[0], w_out[0], w_gate[0], w_up[0],
                        w_down[0])
    yp, cmp_p, sel_p, win_p, pool_p = _prompt_group(x_prompt, g_pre_mix[0], g_post_mix[0], g_pre_ffn[0],
                                                    g_post_ffn[0], wts)
    ys, cmp_s, sel_s, win_s, pool_s = _sample_group(x_sample, cache_kv_cmp[0], cache_kv_sel[0], cache_kv_win[0],
                                                    state_pool[0], page_table, g_pre_mix[0], g_post_mix[0],
                                                    g_pre_ffn[0], g_post_ffn[0], wts)
    return (yp, ys, cmp_p, sel_p, win_p, pool_p, cmp_s, sel_s, win_s, pool_s)
```

```python
import functools

import jax
import jax.numpy as jnp
import numpy as np
from jax import lax
from jax.experimental import pallas as pl
from jax.experimental.pallas import tpu as pltpu

F32 = jnp.float32
MXU_DTYPE = jnp.bfloat16

D_MODEL = 1024
PAGE_SIZE = 128
N_HEADS = 8
KV_HEADS = 2
HPG = N_HEADS // KV_HEADS
HEAD_DIM = 64
ATT_W = N_HEADS * HEAD_DIM
KV_W = KV_HEADS * HEAD_DIM
N_BRANCH = 3
L_CMP = 32
L_SEL = 64
N_SEL = 16
N_FORCED = 3
N_TOP = N_SEL - N_FORCED
WINDOW = 512
SCALE = HEAD_DIM ** -0.5
ROT_DIM = HEAD_DIM // 4
ROT_HALF = ROT_DIM // 2
ROPE_THETA = 500000.0
POOL_WINDOWS = (2, 4, 8, 16)
POOL_W = D_MODEL - ATT_W
POOL_GC = POOL_W // len(POOL_WINDOWS)
POOL_HIST = max(POOL_WINDOWS) - 1
HIST_ROWS = POOL_HIST + 1
N_GATE = N_HEADS * N_BRANCH
D_FF = -(-8 * D_MODEL // (3 * 256)) * 256
EPS = 1e-6
NEG = -1e30

LANES = 128
C_Q = 0
C_KVC = C_Q + ATT_W
C_KVS = C_KVC + 2 * KV_W
C_KVW = C_KVS + 2 * KV_W
C_POOL = C_KVW + 2 * KV_W
C_GATE = C_POOL + POOL_W
IN_W_PAD = C_GATE + LANES

TM_IN = 512
TM_POST = 512
Q_BLK = 256
K_CHUNK = 512
FF_CHUNKS = 2
VMEM_LIMIT = 56 * 1024 * 1024
SAMPLE_VMEM_LIMIT = 60 * 1024 * 1024


def _const_spec(shape):
    nd = len(shape)
    return pl.BlockSpec(shape, lambda *_: (0,) * nd, pipeline_mode=pl.Buffered(1))


def _rms(x, g):
    return x * lax.rsqrt(jnp.mean(x * x, axis=-1, keepdims=True) + EPS) * g


def _rope128(x, c, s1, s2):
    return x * c + pltpu.roll(x, ROT_HALF, 1) * s1 + pltpu.roll(x, LANES - ROT_HALF, 1) * s2


def _dot(a, b):
    return jnp.dot(a, b, preferred_element_type=F32)


def _dot_nt(a, b):
    return lax.dot_general(a, b, (((1,), (1,)), ((), ())), preferred_element_type=F32)


def _dot_tn(a, b):
    return lax.dot_general(a, b, (((0,), (0,)), ((), ())), preferred_element_type=F32)


def _rope_tables(pos):
    pos = jnp.asarray(np.asarray(pos), jnp.int32)
    inv = ROPE_THETA ** (-jnp.arange(ROT_HALF, dtype=F32) * 2.0 / ROT_DIM)
    ang = pos.astype(F32)[:, None] * inv
    cos, sin = jnp.cos(ang), jnp.sin(ang)
    n = pos.shape[0]
    one = jnp.ones((n, HEAD_DIM - ROT_DIM), F32)
    zero = jnp.zeros((n, HEAD_DIM - ROT_DIM), F32)
    zh = jnp.zeros((n, ROT_HALF), F32)
    c = jnp.concatenate([cos, cos, one], axis=1)
    s1 = jnp.concatenate([zh, sin, zero], axis=1)
    s2 = jnp.concatenate([-sin, zh, zero], axis=1)
    tile = lambda t: jnp.concatenate([t, t], axis=1)
    return tile(c), tile(s1), tile(s2)


def _prep_weights(w_in, cmp_pos, w_cmp, w_pool, pool_scale, w_out, w_gate, w_up, w_down):
    w_in_r = jnp.concatenate([
        w_in[:, :ATT_W + 6 * KV_W],
        w_in[:, ATT_W + 6 * KV_W + N_GATE:],
        w_in[:, ATT_W + 6 * KV_W:ATT_W + 6 * KV_W + N_GATE],
        jnp.zeros((D_MODEL, LANES - N_GATE), w_in.dtype)], axis=1).astype(MXU_DTYPE)
    cp = jnp.concatenate([cmp_pos[:, 0], cmp_pos[:, 0], cmp_pos[:, 1], cmp_pos[:, 1]], axis=1)
    wc = jnp.zeros((2 * KV_W, 2 * KV_W), F32)
    for i, c in enumerate((0, 0, 1, 1)):
        wc = wc.at[i * HEAD_DIM:(i + 1) * HEAD_DIM, i * HEAD_DIM:(i + 1) * HEAD_DIM].set(w_cmp[c])
    fc = D_FF // FF_CHUNKS
    return dict(
        w_in=w_in_r, cp=cp, wc=wc.astype(MXU_DTYPE), w_pool=w_pool.astype(MXU_DTYPE),
        pool_scale=pool_scale.reshape(1, POOL_W), w_out=w_out.astype(MXU_DTYPE),
        w_gate=w_gate.reshape(D_MODEL, FF_CHUNKS, fc).transpose(1, 0, 2).astype(MXU_DTYPE),
        w_up=w_up.reshape(D_MODEL, FF_CHUNKS, fc).transpose(1, 0, 2).astype(MXU_DTYPE),
        w_down=w_down.reshape(FF_CHUNKS, fc, D_MODEL).astype(MXU_DTYPE))


def _project_rows(x, g, w_ref, rc, rs1, rs2):
    h = _rms(x, g).astype(MXU_DTYPE)
    z = _dot(h, w_ref[...])
    q = jnp.concatenate(
        [_rope128(z[:, C_Q + i * LANES:C_Q + (i + 1) * LANES], rc, rs1, rs2) for i in range(ATT_W // LANES)],
        axis=1) * SCALE
    kvc = z[:, C_KVC:C_KVS]
    kvs = jnp.concatenate([_rope128(z[:, C_KVS:C_KVS + KV_W], rc, rs1, rs2), z[:, C_KVS + KV_W:C_KVW]], axis=1)
    kvw = jnp.concatenate([_rope128(z[:, C_KVW:C_KVW + KV_W], rc, rs1, rs2), z[:, C_KVW + KV_W:C_POOL]], axis=1)
    u = z[:, C_POOL:C_GATE]
    gates = z[:, C_GATE:IN_W_PAD]
    return q, kvc, kvs, kvw, u, gates


def _compress_rows(kvc, cp, wc_ref, cc, cs1, cs2):
    n = kvc.shape[0] // L_CMP
    summ = jnp.sum(kvc.reshape(n, L_CMP, 2 * KV_W) * cp[None], axis=1)
    ckv = _dot(summ.astype(MXU_DTYPE), wc_ref[...])
    return jnp.concatenate([_rope128(ckv[:, :KV_W], cc, cs1, cs2), ckv[:, KV_W:]], axis=1)


def _pool_windows(zf, rows, cnt):
    outs = []
    for gi, w in enumerate(POOL_WINDOWS):
        a = zf[:, gi * POOL_GC:(gi + 1) * POOL_GC]
        s, sh = a, 1
        while sh < w:
            s = s + pltpu.roll(s, sh, 0)
            sh *= 2
        outs.append(rows(s) / cnt[gi] - rows(a))
    return outs


def _pool_project(d_list, wp_ref, ps):
    y = jnp.concatenate([_dot(d.astype(MXU_DTYPE), wp_ref[gi]) for gi, d in enumerate(d_list)], axis=1)
    return y * ps


def _inproj_prompt_kernel(x_ref, g_ref, w_ref, rc_ref, rs1_ref, rs2_ref, cp_ref, wc_ref, cc_ref, cs1_ref, cs2_ref,
                          wp_ref, ps_ref,
                          q_ref, kvc_ref, kvs_ref, kvw_ref, kvsb_ref, kvwb_ref, gates_ref, pool_ref, ulast_ref,
                          ckv_ref, z_scr):
    j = pl.program_id(1)
    tm = x_ref.shape[0]
    nblk = tm // L_CMP
    r0 = pl.multiple_of(j * tm, tm)
    rows = pl.ds(r0, tm)
    q, kvc, kvs, kvw, u, gates = _project_rows(x_ref[...], g_ref[...], w_ref, rc_ref[rows, :], rs1_ref[rows, :],
                                               rs2_ref[rows, :])
    q_ref[...] = q.astype(q_ref.dtype)
    kvc_ref[0] = kvc.T
    kvs_ref[0] = kvs.T
    kvw_ref[0] = kvw.T
    kvsb_ref[...] = kvs.astype(kvsb_ref.dtype)
    kvwb_ref[...] = kvw.astype(kvwb_ref.dtype)
    gates_ref[...] = gates

    crow = pl.ds(pl.multiple_of(j * nblk, nblk), nblk)
    ckv_ref[...] = _compress_rows(kvc, cp_ref[...], wc_ref, cc_ref[crow, :], cs1_ref[crow, :], cs2_ref[crow, :])

    @pl.when(j == 0)
    def _():
        z_scr[0:HIST_ROWS, :] = jnp.zeros((HIST_ROWS, POOL_W), F32)

    @pl.when(j > 0)
    def _():
        z_scr[0:HIST_ROWS, :] = z_scr[tm:tm + HIST_ROWS, :]

    z_scr[HIST_ROWS:, :] = u
    pos = (r0 + lax.broadcasted_iota(jnp.int32, (tm, 1), 0)).astype(F32)
    cnt = [jnp.minimum(pos + 1.0, float(w)) for w in POOL_WINDOWS]
    d_list = _pool_windows(z_scr[...], lambda a: a[HIST_ROWS:, :], cnt)
    pool_ref[...] = _pool_project(d_list, wp_ref, ps_ref[...]).astype(pool_ref.dtype)
    ulast_ref[0] = u[tm - HIST_ROWS:, :]


def _inproj_prompt(x2, g_pre, wts, seq, batch):
    n = x2.shape[0]
    tm = min(TM_IN, seq)
    nj = seq // tm
    nblk = tm // L_CMP
    rc, rs1, rs2 = _rope_tables(np.arange(seq))
    cc, cs1, cs2 = _rope_tables((np.arange(seq // L_CMP) + 1) * L_CMP - 1)
    row = lambda w: pl.BlockSpec((tm, w), lambda b, j: (b * nj + j, 0))
    out_shape = (
        jax.ShapeDtypeStruct((n, ATT_W), MXU_DTYPE),
        jax.ShapeDtypeStruct((batch, 2 * KV_W, seq), F32),
        jax.ShapeDtypeStruct((batch, 2 * KV_W, seq), F32),
        jax.ShapeDtypeStruct((batch, 2 * KV_W, seq), F32),
        jax.ShapeDtypeStruct((n, 2 * KV_W), MXU_DTYPE),
        jax.ShapeDtypeStruct((n, 2 * KV_W), MXU_DTYPE),
        jax.ShapeDtypeStruct((n, LANES), F32),
        jax.ShapeDtypeStruct((n, POOL_W), MXU_DTYPE),
        jax.ShapeDtypeStruct((batch, HIST_ROWS, POOL_W), F32),
        jax.ShapeDtypeStruct((n // L_CMP, 2 * KV_W), F32),
    )
    dims_major = pl.BlockSpec((1, 2 * KV_W, tm), lambda b, j: (b, 0, j))
    out_specs = (row(ATT_W), dims_major, dims_major, dims_major, row(2 * KV_W), row(2 * KV_W), row(LANES),
                 row(POOL_W), pl.BlockSpec((1, HIST_ROWS, POOL_W), lambda b, j: (b, 0, 0)),
                 pl.BlockSpec((nblk, 2 * KV_W), lambda b, j: (b * nj + j, 0)))
    in_specs = [row(D_MODEL), _const_spec((1, D_MODEL)), _const_spec((D_MODEL, IN_W_PAD)),
                _const_spec((seq, LANES)), _const_spec((seq, LANES)), _const_spec((seq, LANES)),
                _const_spec((L_CMP, 2 * KV_W)), _const_spec((2 * KV_W, 2 * KV_W)),
                _const_spec((seq // L_CMP, LANES)), _const_spec((seq // L_CMP, LANES)),
                _const_spec((seq // L_CMP, LANES)),
                _const_spec((len(POOL_WINDOWS), POOL_GC, POOL_GC)), _const_spec((1, POOL_W))]
    return pl.pallas_call(
        _inproj_prompt_kernel,
        grid=(batch, nj),
        in_specs=in_specs, out_specs=out_specs, out_shape=out_shape,
        scratch_shapes=[pltpu.VMEM((tm + HIST_ROWS, POOL_W), F32)],
        compiler_params=pltpu.CompilerParams(dimension_semantics=("arbitrary", "arbitrary"),
                                             vmem_limit_bytes=VMEM_LIMIT),
    )(x2, g_pre.reshape(1, D_MODEL), wts["w_in"], rc, rs1, rs2, wts["cp"], wts["wc"], cc, cs1, cs2,
      wts["w_pool"], wts["pool_scale"])


def _post_kernel(x_ref, o_ref, p_ref, wo_ref, gpm_ref, gpf_ref, gqf_ref, wg_ref, wu_ref, wd_ref, y_ref):
    mixed = _dot(jnp.concatenate([o_ref[...].astype(MXU_DTYPE), p_ref[...]], axis=1), wo_ref[...])
    x1 = x_ref[...] + _rms(mixed, gpm_ref[...])
    h2 = _rms(x1, gpf_ref[...]).astype(MXU_DTYPE)
    acc = None
    for c in range(FF_CHUNKS):
        f = (jax.nn.silu(_dot(h2, wg_ref[c])) * _dot(h2, wu_ref[c])).astype(MXU_DTYPE)
        part = _dot(f, wd_ref[c])
        acc = part if acc is None else acc + part
    y_ref[...] = x1 + _rms(acc, gqf_ref[...])


def _post(x2, o_att, pool_y, g_post_mix, g_pre_ffn, g_post_ffn, wts):
    n = x2.shape[0]
    tm = min(TM_POST, n)
    fc = D_FF // FF_CHUNKS
    row = lambda w: pl.BlockSpec((tm, w), lambda i: (i, 0))
    return pl.pallas_call(
        _post_kernel,
        grid=(n // tm,),
        in_specs=[row(D_MODEL), row(ATT_W), row(POOL_W), _const_spec((D_MODEL, D_MODEL)),
                  _const_spec((1, D_MODEL)), _const_spec((1, D_MODEL)), _const_spec((1, D_MODEL)),
                  _const_spec((FF_CHUNKS, D_MODEL, fc)), _const_spec((FF_CHUNKS, D_MODEL, fc)),
                  _const_spec((FF_CHUNKS, fc, D_MODEL))],
        out_specs=row(D_MODEL),
        out_shape=jax.ShapeDtypeStruct((n, D_MODEL), F32),
        compiler_params=pltpu.CompilerParams(dimension_semantics=("arbitrary",), vmem_limit_bytes=VMEM_LIMIT),
    )(x2, o_att, pool_y, wts["w_out"], g_post_mix.reshape(1, D_MODEL), g_pre_ffn.reshape(1, D_MODEL),
      g_post_ffn.reshape(1, D_MODEL), wts["w_gate"], wts["w_up"], wts["w_down"])


def _select_blocks_t(p_slc_t, c_row, n_top):
    nb = p_slc_t.shape[0]
    jj = lax.broadcasted_iota(jnp.int32, p_slc_t.shape, 0)
    cand = (jj >= 1) & (jj <= c_row - 2)
    sc = jnp.where(cand, p_slc_t, -1.0)
    rank = jnp.zeros(sc.shape, F32)
    for i in range(nb):
        row = sc[i:i + 1, :]
        beats = (row > sc) | ((row == sc) & (jj > i))
        rank = rank + jnp.where(beats, 1.0, 0.0)
    return (cand & (rank < n_top)) | (jj == 0) | (jj == c_row) | (jj == c_row - 1)


def _lane_tiles(a, n):
    return jnp.concatenate([a] * n, axis=1)


def _attn_prompt_kernel(q_ref, gt_ref, ckv_ref, ks_ref, kw_ref, eneg_ref, o_ref,
                        s_scr, mel_scr, mb_scr, lel_scr, acc_scr):
    qb = pl.program_id(1)
    qblk = q_ref.shape[0]
    rows = HPG * qblk
    ncb = ckv_ref.shape[1]
    half = ncb // 2
    wspan = WINDOW + qblk
    start = qb * qblk
    tpos_c = start + lax.broadcasted_iota(jnp.int32, (qblk, 1), 0)
    tpos_r = start + lax.broadcasted_iota(jnp.int32, (1, qblk), 1)
    gs = jax.nn.sigmoid(gt_ref[...])
    ckv = ckv_ref[0]
    ck = ckv[:, :KV_W].astype(MXU_DTYPE)
    cv = ckv[:, KV_W:].astype(MXU_DTYPE)
    crow = lax.broadcasted_iota(jnp.int32, (ncb, rows), 0)
    cblk = 2 * (crow % half) + crow // half
    cmask = ((cblk + 1) * L_CMP - 1) <= _lane_tiles(tpos_r, HPG)

    ws = pl.multiple_of(jnp.maximum(start - WINDOW, 0), qblk)
    dwin = tpos_c - (ws + lax.broadcasted_iota(jnp.int32, (1, wspan), 1))
    bias_w = jnp.where((dwin >= 0) & (dwin < WINDOW), 0.0, NEG)
    n_chunks = start // K_CHUNK + 1
    lane = lax.broadcasted_iota(jnp.int32, (qblk, LANES), 1)

    outs = [None] * N_HEADS
    for g in range(KV_HEADS):
        pieces = []
        for h in range(HPG):
            hh = g * HPG + h
            tile = q_ref[:, (hh // 2) * LANES:(hh // 2 + 1) * LANES].astype(F32)
            if hh % 2 != g:
                tile = pltpu.roll(tile, HEAD_DIM, 1)
            pieces.append(jnp.where((lane >= g * HEAD_DIM) & (lane < (g + 1) * HEAD_DIM), tile, 0.0))
        qg = jnp.concatenate(pieces, axis=0).astype(MXU_DTYPE)

        s = jnp.where(cmask, _dot_nt(ck, qg), NEG)
        e = jnp.where(cmask, jnp.exp(s - jnp.max(s, axis=0, keepdims=True)), 0.0)
        p = e / jnp.maximum(jnp.sum(e, axis=0, keepdims=True), 1e-30)
        o_c = _dot_tn(p.astype(MXU_DTYPE), cv)
        pc = p[:, 0:qblk]
        for h in range(1, HPG):
            pc = pc + p[:, h * qblk:(h + 1) * qblk]
        sel = _select_blocks_t(pc[0:half] + pc[half:ncb], tpos_r // L_SEL, N_TOP)
        unsel = jnp.concatenate([jnp.where(sel, 0.0, 1.0), jnp.zeros((LANES - half, qblk), F32)], axis=0)
        unsel = unsel.T.astype(MXU_DTYPE)
        qaug = jnp.concatenate([qg, jnp.concatenate([unsel] * HPG, axis=0)], axis=1)

        mel_scr[...] = jnp.full((rows, LANES), NEG, F32)

        def score_chunk(ci, causal):
            k0 = pl.multiple_of(ci * K_CHUNK, K_CHUNK)
            kaug = jnp.concatenate([ks_ref[0, pl.ds(k0, K_CHUNK), 0:KV_W], eneg_ref[pl.ds(k0, K_CHUNK), :]], axis=1)
            sv = _dot_nt(qaug, kaug)
            if causal:
                kpos = k0 + lax.broadcasted_iota(jnp.int32, (1, K_CHUNK), 1)
                bias = jnp.where(kpos <= tpos_c, 0.0, NEG)
                sv = jnp.concatenate([sv[h * qblk:(h + 1) * qblk] + bias for h in range(HPG)], axis=0)
            s_scr[ci] = sv
            mh = sv[:, 0:LANES]
            for j in range(1, K_CHUNK // LANES):
                mh = jnp.maximum(mh, sv[:, j * LANES:(j + 1) * LANES])
            mel_scr[...] = jnp.maximum(mel_scr[...], mh)

        def score_body(ci, _):
            score_chunk(ci, False)
            return 0

        lax.fori_loop(0, n_chunks - 1, score_body, 0)
        score_chunk(n_chunks - 1, True)
        mb_scr[...] = jnp.broadcast_to(jnp.max(mel_scr[...], axis=1, keepdims=True), (rows, LANES))
        lel_scr[...] = jnp.zeros((rows, LANES), F32)
        acc_scr[...] = jnp.zeros((rows, LANES), F32)

        def value_body(ci, _):
            k0 = pl.multiple_of(ci * K_CHUNK, K_CHUNK)
            pv = jnp.exp(s_scr[ci] - _lane_tiles(mb_scr[...], K_CHUNK // LANES))
            lsum = pv[:, 0:LANES]
            for j in range(1, K_CHUNK // LANES):
                lsum = lsum + pv[:, j * LANES:(j + 1) * LANES]
            lel_scr[...] += lsum
            acc_scr[...] += _dot(pv.astype(MXU_DTYPE), ks_ref[0, pl.ds(k0, K_CHUNK), KV_W:2 * KV_W])
            return 0

        lax.fori_loop(0, n_chunks, value_body, 0)
        o_s = acc_scr[...] / jnp.maximum(jnp.sum(lel_scr[...], axis=1, keepdims=True), 1e-30)

        sw = _dot_nt(qg, kw_ref[0, pl.ds(ws, wspan), 0:KV_W])
        sw = jnp.concatenate([sw[h * qblk:(h + 1) * qblk] + bias_w for h in range(HPG)], axis=0)
        ew = jnp.exp(sw - jnp.max(sw, axis=1, keepdims=True))
        o_w = (_dot(ew.astype(MXU_DTYPE), kw_ref[0, pl.ds(ws, wspan), KV_W:2 * KV_W])
               / jnp.maximum(jnp.sum(ew, axis=1, keepdims=True), 1e-30))

        for h in range(HPG):
            hh = g * HPG + h
            col = hh * N_BRANCH
            rs = slice(h * qblk, (h + 1) * qblk)
            cs = slice(g * HEAD_DIM, (g + 1) * HEAD_DIM)
            outs[hh] = (gs[:, col:col + 1] * o_c[rs, cs] + gs[:, col + 1:col + 2] * o_s[rs, cs]
                        + gs[:, col + 2:col + 3] * o_w[rs, cs])
    o_ref[...] = jnp.concatenate(outs, axis=1).astype(o_ref.dtype)


def _attn_prompt(q, gates, ckv_perm, kvsb, kvwb, seq, batch):
    n = q.shape[0]
    qblk = min(Q_BLK, seq)
    nqb = seq // qblk
    nb = seq // L_SEL
    nchunk = seq // K_CHUNK
    rows = HPG * qblk
    assert seq >= WINDOW + qblk and seq % K_CHUNK == 0 and K_CHUNK % qblk == 0
    assert nb <= LANES
    eneg = jnp.asarray(NEG * (np.arange(seq)[:, None] // L_SEL == np.arange(LANES)[None, :]), MXU_DTYPE)
    row = lambda w: pl.BlockSpec((qblk, w), lambda b, i: (b * nqb + i, 0))
    per_b = lambda r, w: pl.BlockSpec((1, r, w), lambda b, i: (b, 0, 0))
    return pl.pallas_call(
        _attn_prompt_kernel,
        grid=(batch, nqb),
        in_specs=[row(ATT_W), row(LANES), per_b(seq // L_CMP, 2 * KV_W), per_b(seq, 2 * KV_W),
                  per_b(seq, 2 * KV_W), _const_spec((seq, LANES))],
        out_specs=row(ATT_W),
        out_shape=jax.ShapeDtypeStruct((n, ATT_W), MXU_DTYPE),
        scratch_shapes=[pltpu.VMEM((nchunk, rows, K_CHUNK), F32), pltpu.VMEM((rows, LANES), F32),
                        pltpu.VMEM((rows, LANES), F32), pltpu.VMEM((rows, LANES), F32),
                        pltpu.VMEM((rows, LANES), F32)],
        compiler_params=pltpu.CompilerParams(dimension_semantics=("arbitrary", "arbitrary"),
                                             vmem_limit_bytes=VMEM_LIMIT),
    )(q, gates, ckv_perm, kvsb.reshape(batch, seq, 2 * KV_W), kvwb.reshape(batch, seq, 2 * KV_W), eneg)


def _prompt_group(x_prompt, g_pre_mix, g_post_mix, g_pre_ffn, g_post_ffn, wts):
    batch, seq, _ = x_prompt.shape
    x2 = x_prompt.reshape(batch * seq, D_MODEL)
    q, kvc, kvs, kvw, kvsb, kvwb, gates, pool_y, ulast, ckv = _inproj_prompt(x2, g_pre_mix, wts, seq, batch)
    ncb = seq // L_CMP
    ckv_perm = ckv.reshape(batch, ncb // 2, 2, 2 * KV_W).transpose(0, 2, 1, 3).reshape(batch, ncb, 2 * KV_W)
    o_att = _attn_prompt(q, gates, ckv_perm, kvsb, kvwb, seq, batch)
    y = _post(x2, o_att, pool_y, g_post_mix, g_pre_ffn, g_post_ffn, wts)
    wkeep = min(WINDOW, seq)
    kv6 = lambda a: jnp.transpose(a.reshape(1, batch, 2, KV_HEADS, HEAD_DIM, a.shape[-1]), (0, 1, 5, 2, 3, 4))
    return (y.reshape(batch, seq, D_MODEL), kv6(kvc), kv6(kvs), kv6(kvw[:, :, seq - wkeep:]),
            ulast[None, :, 1:, :])


def _inproj_sample_kernel(x_ref, g_ref, w_ref, rc_ref, rs1_ref, rs2_ref, cp_ref, wc_ref, cc_ref, cs1_ref, cs2_ref,
                          wp_ref, ps_ref, hist_ref,
                          q_ref, kvc_ref, kvs_ref, kvw_ref, gates_ref, pool_ref, znew_ref, ckv_ref, z_scr, *, past):
    nseq, zrows, _ = z_scr.shape
    t_new = zrows - HIST_ROWS
    n = nseq * t_new
    q, kvc, kvs, kvw, u, gates = _project_rows(x_ref[...], g_ref[...], w_ref, rc_ref[...], rs1_ref[...],
                                               rs2_ref[...])
    q_ref[...] = q
    kvc_ref[...] = kvc
    kvs_ref[...] = kvs
    kvw_ref[...] = kvw
    gates_ref[...] = gates

    summ = jnp.sum(kvc.reshape(nseq, t_new, 2 * KV_W) * cp_ref[0:t_new, :][None], axis=1)
    ckv = _dot(summ.astype(MXU_DTYPE), wc_ref[...])
    ckv_ref[...] = jnp.concatenate([_rope128(ckv[:, :KV_W], cc_ref[...], cs1_ref[...], cs2_ref[...]),
                                    ckv[:, KV_W:]], axis=1)

    z_scr[:, 0:HIST_ROWS, :] = hist_ref[...]
    z_scr[:, HIST_ROWS:, :] = u.reshape(nseq, t_new, POOL_W)
    zf = z_scr[...].reshape(nseq * zrows, POOL_W)
    tok = lax.broadcasted_iota(jnp.int32, (n, 1), 0) % t_new
    pos = (past + tok).astype(F32)
    cnt = [jnp.minimum(pos + 1.0, float(w)) for w in POOL_WINDOWS]
    take = lambda a: a.reshape(nseq, zrows, POOL_GC)[:, HIST_ROWS:, :].reshape(n, POOL_GC)
    d_list = _pool_windows(zf, take, cnt)
    pool_ref[...] = _pool_project(d_list, wp_ref, ps_ref[...]).astype(pool_ref.dtype)
    znew_ref[...] = z_scr[:, zrows - HIST_ROWS:, :]


def _inproj_sample(x2, g_pre, wts, hist16, nseq, t_new, past):
    n = x2.shape[0]
    rc, rs1, rs2 = _rope_tables(np.tile(past + np.arange(t_new), nseq))
    cc, cs1, cs2 = _rope_tables(np.array([past + L_CMP - 1]))
    full = lambda *s: pl.BlockSpec(s, lambda i: (0,) * len(s))
    out_shape = (
        jax.ShapeDtypeStruct((n, ATT_W), F32),
        jax.ShapeDtypeStruct((n, 2 * KV_W), F32),
        jax.ShapeDtypeStruct((n, 2 * KV_W), F32),
        jax.ShapeDtypeStruct((n, 2 * KV_W), F32),
        jax.ShapeDtypeStruct((n, LANES), F32),
        jax.ShapeDtypeStruct((n, POOL_W), MXU_DTYPE),
        jax.ShapeDtypeStruct((nseq, HIST_ROWS, POOL_W), F32),
        jax.ShapeDtypeStruct((nseq, 2 * KV_W), F32),
    )
    out_specs = (full(n, ATT_W), full(n, 2 * KV_W), full(n, 2 * KV_W), full(n, 2 * KV_W), full(n, LANES),
                 full(n, POOL_W), full(nseq, HIST_ROWS, POOL_W), full(nseq, 2 * KV_W))
    in_specs = [full(n, D_MODEL), full(1, D_MODEL), full(D_MODEL, IN_W_PAD), full(n, LANES), full(n, LANES),
                full(n, LANES), full(L_CMP, 2 * KV_W), full(2 * KV_W, 2 * KV_W), full(1, LANES), full(1, LANES),
                full(1, LANES), full(len(POOL_WINDOWS), POOL_GC, POOL_GC), full(1, POOL_W),
                full(nseq, HIST_ROWS, POOL_W)]
    return pl.pallas_call(
        functools.partial(_inproj_sample_kernel, past=past),
        grid=(1,),
        in_specs=in_specs, out_specs=out_specs, out_shape=out_shape,
        scratch_shapes=[pltpu.VMEM((nseq, HIST_ROWS + t_new, POOL_W), F32)],
        compiler_params=pltpu.CompilerParams(dimension_semantics=("arbitrary",), vmem_limit_bytes=VMEM_LIMIT),
    )(x2, g_pre.reshape(1, D_MODEL), wts["w_in"], rc, rs1, rs2, wts["cp"], wts["wc"], cc, cs1, cs2,
      wts["w_pool"], wts["pool_scale"], hist16)


PAGE_CHUNK = 8
CMP_LANE_BLKS = PAGE_SIZE // L_CMP
HALF_PAGES = LANES // CMP_LANE_BLKS
CMP_PAGES_PER_STEP = 2


def _sample_tables(past, t_new, wb):
    npages = past // PAGE_SIZE
    nrow = HPG * KV_HEADS * t_new
    r = np.arange(nrow)
    t_row = r % t_new
    pos = past + t_row
    lanes = np.arange(npages * CMP_LANE_BLKS)
    page = HALF_PAGES * (lanes // LANES) + lanes % HALF_PAGES
    m = (lanes % LANES) // HALF_PAGES
    blk = CMP_LANE_BLKS * page + (m - 1) % CMP_LANE_BLKS
    cmask = (blk[None, :] + 1) * L_CMP - 1 <= pos[:, None]
    n_new = 2 * t_new
    blk_new = past // L_CMP + np.arange(n_new)
    n_cmp = (past + -(-t_new // L_SEL) * L_SEL) // L_CMP
    cmask_new = ((blk_new[None, :] + 1) * L_CMP - 1 <= pos[:, None]) & (blk_new[None, :] < n_cmp)
    jidx = np.where(m == 1, 2 * page, np.where(m == 3, 2 * page + 1, -1))
    rg = np.arange(KV_HEADS * t_new)
    c = ((past + rg % t_new) // L_SEL)[:, None]
    jj = jidx[None, :]
    cand = (jj >= 1) & (jj <= c - 2)
    forced = (jj >= 0) & ((jj == 0) | (jj == c) | (jj == c - 1))
    l = np.arange(n_new)
    new_blk = past // L_SEL
    c_row = (pos // L_SEL)[:, None]
    new_forced = (new_blk == 0) | (new_blk == c_row) | (new_blk == c_row - 1)
    nmask = new_forced & (l[None, :] <= t_row[:, None]) & (l[None, :] < t_new)
    i = np.arange(wb)
    d = t_row[:, None] + wb - i[None, :]
    wmask = (d >= 0) & (d < WINDOW) & (past - wb + i[None, :] >= 0)
    dn = t_row[:, None] - l[None, :]
    wmask_new = (dn >= 0) & (dn < WINDOW) & (l[None, :] < t_new)
    ck = PAGE_CHUNK * PAGE_SIZE
    k = np.arange(ck)
    eeo = np.zeros((npages // PAGE_CHUNK, 2 * npages, ck), np.float32)
    for ch in range(npages // PAGE_CHUNK):
        pg = PAGE_CHUNK * ch + k // PAGE_SIZE
        upper = (k % PAGE_SIZE) >= L_SEL
        eeo[ch, pg[~upper], k[~upper]] = 1.0
        eeo[ch, npages + pg[upper], k[upper]] = 1.0
    tt = np.arange(PAGE_SIZE)
    steps = HALF_PAGES // CMP_PAGES_PER_STEP
    gsum = np.zeros((steps, 2 * CMP_PAGES_PER_STEP * PAGE_SIZE, LANES), np.float32)
    for it in range(steps):
        for rep in range(2):
            for kk in range(CMP_PAGES_PER_STEP):
                r0 = (rep * CMP_PAGES_PER_STEP + kk) * PAGE_SIZE
                col = HALF_PAGES * ((tt // L_CMP + 1) % CMP_LANE_BLKS) + it * CMP_PAGES_PER_STEP + kk
                gsum[it, r0 + tt, col] = 1.0
    f = lambda a: jnp.asarray(np.asarray(a, np.float32))
    return dict(gsum=jnp.asarray(gsum, MXU_DTYPE), cmask=f(cmask), cmask_new=f(cmask_new), cand=f(cand), forced=f(forced), nmask=f(nmask),
                wmask=f(wmask), wmask_new=f(wmask_new), jidx=jnp.asarray(jidx[None, :], jnp.int32),
                eeo=jnp.asarray(eeo, MXU_DTYPE), blk=blk, jidx_np=jidx)


def _rope_tables_t(pos):
    c, s1, s2 = _rope_tables(pos)
    return c.T, s1.T, s2.T


def _attn_sample_kernel(pt_ref, q_ref, gt_ref, kvs_ref, kvw_ref, ckv0_ref, cw_ref, cmp_hbm, sel_hbm,
                        cpt_ref, wct_ref, cc_ref, cs1_ref, cs2_ref, cmask_ref, cmaskn_ref, cand_ref, forced_ref,
                        nmask_ref, wmask_ref, wmaskn_ref, jidx_ref, eeo_ref, gsum_ref,
                        o_ref,
                        cbuf, sbuf, sem, summ_scr, s_scr, *, jidx_np, n_top):
    b = pl.program_id(0)
    nseq = pl.num_programs(0)
    npages = pt_ref.shape[1]
    t_new = q_ref.shape[1]
    nrow = HPG * KV_HEADS * t_new
    ngt = KV_HEADS * t_new
    slot = b % 2

    def copies(seq, sl):
        out = []
        for p in range(npages):
            pg = pt_ref[seq, p]
            out.append(pltpu.make_async_copy(cmp_hbm.at[pg], cbuf.at[sl, p], sem.at[0, sl]))
            out.append(pltpu.make_async_copy(sel_hbm.at[pg], sbuf.at[sl, p], sem.at[1, sl]))
        return out

    @pl.when(b == 0)
    def _():
        for cpy in copies(b, slot):
            cpy.start()

    @pl.when(b + 1 < nseq)
    def _():
        for cpy in copies(b + 1, 1 - slot):
            cpy.start()

    qb = q_ref[0]
    zeros = jnp.zeros((t_new, HEAD_DIM), F32)
    blocks = []
    for h in range(HPG):
        for g in range(KV_HEADS):
            piece = qb[:, (g * HPG + h) * HEAD_DIM:(g * HPG + h + 1) * HEAD_DIM]
            blocks.append(jnp.concatenate([piece, zeros] if g == 0 else [zeros, piece], axis=1))
    qbd = jnp.concatenate(blocks, axis=0).astype(MXU_DTYPE)

    def new_rows(rows):
        a = jnp.concatenate([rows, jnp.zeros((2 * t_new - rows.shape[0], 2 * KV_W), F32)], axis=0)
        return a[:, :KV_W].astype(MXU_DTYPE), a[:, KV_W:].astype(MXU_DTYPE)

    def joint_softmax(s1, ok1, s2, ok2):
        s1 = jnp.where(ok1, s1, NEG)
        s2 = jnp.where(ok2, s2, NEG)
        m = jnp.maximum(jnp.max(s1, axis=1, keepdims=True), jnp.max(s2, axis=1, keepdims=True))
        e1 = jnp.where(ok1, jnp.exp(s1 - m), 0.0)
        e2 = jnp.where(ok2, jnp.exp(s2 - m), 0.0)
        l = jnp.maximum(jnp.sum(e1, axis=1, keepdims=True) + jnp.sum(e2, axis=1, keepdims=True), 1e-30)
        return e1 / l, e2 / l

    kwn, vwn = new_rows(kvw_ref[0])
    p1, p2 = joint_softmax(_dot(qbd, cw_ref[0, 0:KV_W, :].astype(MXU_DTYPE)), wmask_ref[...] > 0.5,
                           _dot_nt(qbd, kwn), wmaskn_ref[...] > 0.5)
    o_w = (_dot_nt(p1.astype(MXU_DTYPE), cw_ref[0, KV_W:2 * KV_W, :].astype(MXU_DTYPE))
           + _dot(p2.astype(MXU_DTYPE), vwn))

    for cpy in copies(b, slot):
        cpy.wait()

    nhalf = npages // HALF_PAGES
    summ_scr[...] = jnp.zeros(summ_scr.shape, F32)

    def cmp_body(it, _):
        his, los = [], []
        for k in range(CMP_PAGES_PER_STEP):
            pg = it * CMP_PAGES_PER_STEP + k
            x = jnp.concatenate([cbuf[slot, h2 * HALF_PAGES + pg] * cpt_ref[...] for h2 in range(nhalf)], axis=0)
            hi = x.astype(MXU_DTYPE)
            his.append(hi)
            los.append((x - hi.astype(F32)).astype(MXU_DTYPE))
        summ_scr[...] += _dot(jnp.concatenate(his + los, axis=1), gsum_ref[it])
        return 0

    lax.fori_loop(0, HALF_PAGES // CMP_PAGES_PER_STEP, cmp_body, 0)
    summ = jnp.concatenate([summ_scr[h2 * 2 * KV_W:(h2 + 1) * 2 * KV_W] for h2 in range(nhalf)], axis=1)
    ckv = _dot(wct_ref[...], summ.astype(MXU_DTYPE))
    kc = ckv[0:KV_W]
    ckt = (kc * cc_ref[...] + pltpu.roll(kc, ROT_HALF, 0) * cs1_ref[...]
           + pltpu.roll(kc, KV_W - ROT_HALF, 0) * cs2_ref[...]).astype(MXU_DTYPE)
    cvt = ckv[KV_W:2 * KV_W].astype(MXU_DTYPE)

    ckn, cvn = new_rows(ckv0_ref[0])
    p1, p2 = joint_softmax(_dot(qbd, ckt), cmask_ref[...] > 0.5, _dot_nt(qbd, ckn), cmaskn_ref[...] > 0.5)
    o_c = _dot_nt(p1.astype(MXU_DTYPE), cvt) + _dot(p2.astype(MXU_DTYPE), cvn)
    pc = p1[0:ngt]
    for h in range(1, HPG):
        pc = pc + p1[h * ngt:(h + 1) * ngt]
    pair = jnp.concatenate(
        [pc[:, t * LANES:(t + 1) * LANES] + pltpu.roll(pc[:, t * LANES:(t + 1) * LANES], LANES - HALF_PAGES, 1)
         for t in range(pc.shape[1] // LANES)], axis=1)

    cand = cand_ref[...] > 0.5
    sc = jnp.where(cand, pair, -1.0)
    jl = jidx_ref[...]
    rank = jnp.zeros(sc.shape, F32)
    for i in np.nonzero(jidx_np >= 0)[0]:
        col = sc[:, int(i):int(i) + 1]
        beats = (col > sc) | ((col == sc) & (jl > int(jidx_np[i])))
        rank = rank + jnp.where(beats, 1.0, 0.0)
    sel = jnp.where((cand & (rank < n_top)) | (forced_ref[...] > 0.5), 1.0, 0.0)
    sel_eo = jnp.concatenate(
        [sel[:, t * LANES + HALF_PAGES:t * LANES + 2 * HALF_PAGES] for t in range(sel.shape[1] // LANES)]
        + [sel[:, t * LANES + 3 * HALF_PAGES:(t + 1) * LANES] for t in range(sel.shape[1] // LANES)],
        axis=1).astype(MXU_DTYPE)

    def score_body(ch, mel):
        keep = _dot(sel_eo, eeo_ref[ch]) > 0.5
        bias = jnp.where(keep, 0.0, NEG)
        bias = jnp.concatenate([bias] * HPG, axis=0)
        for pp in range(0, PAGE_CHUNK, 2):
            pg = ch * PAGE_CHUNK + pp
            kt = jnp.concatenate([sbuf[slot, pg, 0:KV_W, :], sbuf[slot, pg + 1, 0:KV_W, :]], axis=1)
            sv = _dot(qbd, kt.astype(MXU_DTYPE)) + bias[:, pp * LANES:(pp + 2) * LANES]
            s_scr[pg] = sv[:, 0:LANES]
            s_scr[pg + 1] = sv[:, LANES:2 * LANES]
            mel = jnp.maximum(mel, jnp.maximum(sv[:, 0:LANES], sv[:, LANES:2 * LANES]))
        return mel

    mel = lax.fori_loop(0, npages // PAGE_CHUNK, score_body, jnp.full((nrow, LANES), NEG, F32))
    ksn, vsn = new_rows(kvs_ref[0])
    nok = nmask_ref[...] > 0.5
    s_new = jnp.where(nok, _dot_nt(qbd, ksn), NEG)
    m = jnp.maximum(jnp.max(mel, axis=1, keepdims=True), jnp.max(s_new, axis=1, keepdims=True))
    mb = jnp.broadcast_to(m, (nrow, LANES))

    def value_body(ch, carry):
        lel, acc = carry
        for pp in range(0, PAGE_CHUNK, 2):
            pg = ch * PAGE_CHUNK + pp
            pv0 = jnp.exp(s_scr[pg] - mb)
            pv1 = jnp.exp(s_scr[pg + 1] - mb)
            lel = lel + (pv0 + pv1)
            vt = jnp.concatenate([sbuf[slot, pg, KV_W:2 * KV_W, :], sbuf[slot, pg + 1, KV_W:2 * KV_W, :]], axis=1)
            acc = acc + _dot_nt(jnp.concatenate([pv0, pv1], axis=1).astype(MXU_DTYPE), vt.astype(MXU_DTYPE))
        return lel, acc

    e_new = jnp.where(nok, jnp.exp(s_new - m), 0.0)
    lel, acc = lax.fori_loop(0, npages // PAGE_CHUNK, value_body,
                             (jnp.zeros((nrow, LANES), F32), _dot(e_new.astype(MXU_DTYPE), vsn)))
    l = jnp.sum(lel, axis=1, keepdims=True) + jnp.sum(e_new, axis=1, keepdims=True)
    o_s = acc / jnp.maximum(l, 1e-30)

    gs = jax.nn.sigmoid(gt_ref[0])
    outs = [None] * N_HEADS
    for h in range(HPG):
        for g in range(KV_HEADS):
            hh = g * HPG + h
            rs = slice((h * KV_HEADS + g) * t_new, (h * KV_HEADS + g + 1) * t_new)
            cs = slice(g * HEAD_DIM, (g + 1) * HEAD_DIM)
            col = hh * N_BRANCH
            outs[hh] = (gs[:, col:col + 1] * o_c[rs, cs] + gs[:, col + 1:col + 2] * o_s[rs, cs]
                        + gs[:, col + 2:col + 3] * o_w[rs, cs])
    o_ref[0] = jnp.concatenate(outs, axis=1)


def _dims_major(a):
    n, toks = a.shape[:2]
    return jnp.transpose(a, (0, 2, 3, 4, 1)).reshape(n, 2 * KV_W, toks)


def _attn_sample(q, gates, kvs, kvw, ckv0, cache_cmp, cache_sel, cache_win, page_table, wts, t_new):
    nseq, npages = page_table.shape
    past = npages * PAGE_SIZE
    wb = cache_win.shape[1]
    nrow = HPG * KV_HEADS * t_new
    ngt = KV_HEADS * t_new
    nlane = npages * CMP_LANE_BLKS
    assert t_new <= L_CMP and past % L_SEL == 0 and wb + t_new >= WINDOW
    assert npages % HALF_PAGES == 0 and npages % PAGE_CHUNK == 0 and 2 * npages == LANES
    tb = _sample_tables(past, t_new, wb)
    cc, cs1, cs2 = _rope_tables_t((tb["blk"] + 1) * L_CMP - 1)
    cpt = jnp.tile(wts["cp"].T, (1, CMP_LANE_BLKS))
    seq3 = lambda r, w: pl.BlockSpec((1, r, w), lambda b, pt: (b, 0, 0))
    const = lambda *s: pl.BlockSpec(s, lambda b, pt: (0,) * len(s), pipeline_mode=pl.Buffered(1))
    hbm = pl.BlockSpec(memory_space=pl.ANY)
    in_specs = [seq3(t_new, ATT_W), seq3(t_new, LANES), seq3(t_new, 2 * KV_W), seq3(t_new, 2 * KV_W),
                seq3(1, 2 * KV_W), seq3(2 * KV_W, wb), hbm, hbm,
                const(2 * KV_W, PAGE_SIZE), const(2 * KV_W, 2 * KV_W), const(KV_W, nlane), const(KV_W, nlane),
                const(KV_W, nlane), const(nrow, nlane), const(nrow, 2 * t_new), const(ngt, nlane),
                const(ngt, nlane), const(nrow, 2 * t_new), const(nrow, wb), const(nrow, 2 * t_new),
                const(1, nlane), const(npages // PAGE_CHUNK, 2 * npages, PAGE_CHUNK * PAGE_SIZE),
                const(HALF_PAGES // CMP_PAGES_PER_STEP, 2 * CMP_PAGES_PER_STEP * PAGE_SIZE, LANES)]
    grid_spec = pltpu.PrefetchScalarGridSpec(
        num_scalar_prefetch=1, grid=(nseq,), in_specs=in_specs,
        out_specs=seq3(t_new, ATT_W),
        scratch_shapes=[pltpu.VMEM((2, npages, 2 * KV_W, PAGE_SIZE), F32),
                        pltpu.VMEM((2, npages, 2 * KV_W, PAGE_SIZE), F32),
                        pltpu.SemaphoreType.DMA((2, 2)),
                        pltpu.VMEM((nlane // LANES * 2 * KV_W, LANES), F32),
                        pltpu.VMEM((npages, nrow, PAGE_SIZE), F32)])
    return pl.pallas_call(
        functools.partial(_attn_sample_kernel, jidx_np=tb["jidx_np"], n_top=N_TOP),
        grid_spec=grid_spec,
        out_shape=jax.ShapeDtypeStruct((nseq, t_new, ATT_W), F32),
        compiler_params=pltpu.CompilerParams(dimension_semantics=("arbitrary",),
                                             vmem_limit_bytes=SAMPLE_VMEM_LIMIT),
    )(page_table, q.reshape(nseq, t_new, ATT_W), gates.reshape(nseq, t_new, LANES),
      kvs.reshape(nseq, t_new, 2 * KV_W), kvw.reshape(nseq, t_new, 2 * KV_W), ckv0.reshape(nseq, 1, 2 * KV_W),
      _dims_major(cache_win), _dims_major(cache_cmp), _dims_major(cache_sel), cpt, wts["wc"].T, cc, cs1, cs2,
      tb["cmask"], tb["cmask_new"], tb["cand"], tb["forced"], tb["nmask"], tb["wmask"], tb["wmask_new"],
      tb["jidx"], tb["eeo"], tb["gsum"])


def _sample_group(x_sample, cache_cmp, cache_sel, cache_win, state_pool, page_table, g_pre_mix, g_post_mix,
                  g_pre_ffn, g_post_ffn, wts):
    nseq, t_new, _ = x_sample.shape
    past = page_table.shape[1] * PAGE_SIZE
    wb = cache_win.shape[1]
    x2 = x_sample.reshape(nseq * t_new, D_MODEL)
    hist16 = jnp.pad(state_pool, ((0, 0), (HIST_ROWS - POOL_HIST, 0), (0, 0)))
    q, kvc, kvs, kvw, gates, pool_y, znew, ckv0 = _inproj_sample(x2, g_pre_mix, wts, hist16, nseq, t_new, past)
    o_att = _attn_sample(q, gates, kvs, kvw, ckv0, cache_cmp, cache_sel, cache_win, page_table, wts, t_new)
    y = _post(x2, o_att.reshape(nseq * t_new, ATT_W), pool_y, g_post_mix, g_pre_ffn, g_post_ffn, wts)
    kv6 = lambda a: a.reshape(1, nseq, t_new, 2, KV_HEADS, HEAD_DIM)
    wkeep = min(WINDOW, wb + t_new)
    win_new = jnp.concatenate([cache_win[:, wb + t_new - wkeep:], kv6(kvw)[0]], axis=1)[None]
    return (y.reshape(nseq, t_new, D_MODEL), kv6(kvc), kv6(kvs), win_new, znew[None, :, 1:, :])


def kernel(x_prompt, x_sample, cache_kv_cmp, cache_kv_sel, cache_kv_win, state_pool, page_table, g_pre_mix,
           g_post_mix, g_pre_ffn, g_post_ffn, w_in, cmp_pos, w_cmp, w_pool, pool_scale, w_out, w_gate, w_up, w_down):
    wts = _prep_weights(w_in[0], cmp_pos[0], w_cmp[0], w_pool[0], pool_scale[0], w_out[0], w_gate[0], w_up[0],
                        w_down[0])
    yp, cmp_p, sel_p, win_p, pool_p = _prompt_group(x_prompt, g_pre_mix[0], g_post_mix[0], g_pre_ffn[0],
                                                    g_post_ffn[0], wts)
    ys, cmp_s, sel_s, win_s, pool_s = _sample_group(x_sample, cache_kv_cmp[0], cache_kv_sel[0], cache_kv_win[0],
                                                    state_pool[0], page_table, g_pre_mix[0], g_post_mix[0],
                                                    g_pre_ffn[0], g_post_ffn[0], wts)
    return (yp, ys, cmp_p, sel_p, win_p, pool_p, cmp_s, sel_s, win_s, pool_s)
```

```python
import functools

import jax
import jax.numpy as jnp
import numpy as np
from jax import lax
from jax.experimental import pallas as pl
from jax.experimental.pallas import tpu as pltpu

F32 = jnp.float32
MXU_DTYPE = jnp.bfloat16

D_MODEL = 1024
PAGE_SIZE = 128
N_HEADS = 8
KV_HEADS = 2
HPG = N_HEADS // KV_HEADS
HEAD_DIM = 64
ATT_W = N_HEADS * HEAD_DIM
KV_W = KV_HEADS * HEAD_DIM
N_BRANCH = 3
L_CMP = 32
L_SEL = 64
N_SEL = 16
N_FORCED = 3
N_TOP = N_SEL - N_FORCED
WINDOW = 512
SCALE = HEAD_DIM ** -0.5
ROT_DIM = HEAD_DIM // 4
ROT_HALF = ROT_DIM // 2
ROPE_THETA = 500000.0
POOL_WINDOWS = (2, 4, 8, 16)
POOL_W = D_MODEL - ATT_W
POOL_GC = POOL_W // len(POOL_WINDOWS)
POOL_HIST = max(POOL_WINDOWS) - 1
HIST_ROWS = POOL_HIST + 1
N_GATE = N_HEADS * N_BRANCH
D_FF = -(-8 * D_MODEL // (3 * 256)) * 256
EPS = 1e-6
NEG = -1e30

LANES = 128
C_Q = 0
C_KVC = C_Q + ATT_W
C_KVS = C_KVC + 2 * KV_W
C_KVW = C_KVS + 2 * KV_W
C_POOL = C_KVW + 2 * KV_W
C_GATE = C_POOL + POOL_W
IN_W_PAD = C_GATE + LANES

TM_IN = 512
TM_POST = 512
Q_BLK = 256
K_CHUNK = 512
FF_CHUNKS = 2
VMEM_LIMIT = 56 * 1024 * 1024
SAMPLE_VMEM_LIMIT = 60 * 1024 * 1024


def _const_spec(shape):
    nd = len(shape)
    return pl.BlockSpec(shape, lambda *_: (0,) * nd, pipeline_mode=pl.Buffered(1))


def _rms(x, g):
    return x * lax.rsqrt(jnp.mean(x * x, axis=-1, keepdims=True) + EPS) * g


def _rope128(x, c, s1, s2):
    return x * c + pltpu.roll(x, ROT_HALF, 1) * s1 + pltpu.roll(x, LANES - ROT_HALF, 1) * s2


def _dot(a, b):
    return jnp.dot(a, b, preferred_element_type=F32)


def _dot_nt(a, b):
    return lax.dot_general(a, b, (((1,), (1,)), ((), ())), preferred_element_type=F32)


def _dot_tn(a, b):
    return lax.dot_general(a, b, (((0,), (0,)), ((), ())), preferred_element_type=F32)


def _rope_tables(pos):
    pos = jnp.asarray(np.asarray(pos), jnp.int32)
    inv = ROPE_THETA ** (-jnp.arange(ROT_HALF, dtype=F32) * 2.0 / ROT_DIM)
    ang = pos.astype(F32)[:, None] * inv
    cos, sin = jnp.cos(ang), jnp.sin(ang)
    n = pos.shape[0]
    one = jnp.ones((n, HEAD_DIM - ROT_DIM), F32)
    zero = jnp.zeros((n, HEAD_DIM - ROT_DIM), F32)
    zh = jnp.zeros((n, ROT_HALF), F32)
    c = jnp.concatenate([cos, cos, one], axis=1)
    s1 = jnp.concatenate([zh, sin, zero], axis=1)
    s2 = jnp.concatenate([-sin, zh, zero], axis=1)
    tile = lambda t: jnp.concatenate([t, t], axis=1)
    return tile(c), tile(s1), tile(s2)


def _prep_weights(w_in, cmp_pos, w_cmp, w_pool, pool_scale, w_out, w_gate, w_up, w_down):
    w_in_r = jnp.concatenate([
        w_in[:, :ATT_W + 6 * KV_W],
        w_in[:, ATT_W + 6 * KV_W + N_GATE:],
        w_in[:, ATT_W + 6 * KV_W:ATT_W + 6 * KV_W + N_GATE],
        jnp.zeros((D_MODEL, LANES - N_GATE), w_in.dtype)], axis=1).astype(MXU_DTYPE)
    cp = jnp.concatenate([cmp_pos[:, 0], cmp_pos[:, 0], cmp_pos[:, 1], cmp_pos[:, 1]], axis=1)
    wc = jnp.zeros((2 * KV_W, 2 * KV_W), F32)
    for i, c in enumerate((0, 0, 1, 1)):
        wc = wc.at[i * HEAD_DIM:(i + 1) * HEAD_DIM, i * HEAD_DIM:(i + 1) * HEAD_DIM].set(w_cmp[c])
    fc = D_FF // FF_CHUNKS
    return dict(
        w_in=w_in_r, cp=cp, wc=wc.astype(MXU_DTYPE), w_pool=w_pool.astype(MXU_DTYPE),
        pool_scale=pool_scale.reshape(1, POOL_W), w_out=w_out.astype(MXU_DTYPE),
        w_gate=w_gate.reshape(D_MODEL, FF_CHUNKS, fc).transpose(1, 0, 2).astype(MXU_DTYPE),
        w_up=w_up.reshape(D_MODEL, FF_CHUNKS, fc).transpose(1, 0, 2).astype(MXU_DTYPE),
        w_down=w_down.reshape(FF_CHUNKS, fc, D_MODEL).astype(MXU_DTYPE))


def _project_rows(x, g, w_ref, rc, rs1, rs2):
    h = _rms(x, g).astype(MXU_DTYPE)
    z = _dot(h, w_ref[...])
    q = jnp.concatenate(
        [_rope128(z[:, C_Q + i * LANES:C_Q + (i + 1) * LANES], rc, rs1, rs2) for i in range(ATT_W // LANES)],
        axis=1) * SCALE
    kvc = z[:, C_KVC:C_KVS]
    kvs = jnp.concatenate([_rope128(z[:, C_KVS:C_KVS + KV_W], rc, rs1, rs2), z[:, C_KVS + KV_W:C_KVW]], axis=1)
    kvw = jnp.concatenate([_rope128(z[:, C_KVW:C_KVW + KV_W], rc, rs1, rs2), z[:, C_KVW + KV_W:C_POOL]], axis=1)
    u = z[:, C_POOL:C_GATE]
    gates = z[:, C_GATE:IN_W_PAD]
    return q, kvc, kvs, kvw, u, gates


def _compress_rows(kvc, cp, wc_ref, cc, cs1, cs2):
    n = kvc.shape[0] // L_CMP
    summ = jnp.sum(kvc.reshape(n, L_CMP, 2 * KV_W) * cp[None], axis=1)
    ckv = _dot(summ.astype(MXU_DTYPE), wc_ref[...])
    return jnp.concatenate([_rope128(ckv[:, :KV_W], cc, cs1, cs2), ckv[:, KV_W:]], axis=1)


def _pool_windows(zf, rows, cnt):
    outs = []
    for gi, w in enumerate(POOL_WINDOWS):
        a = zf[:, gi * POOL_GC:(gi + 1) * POOL_GC]
        s, sh = a, 1
        while sh < w:
            s = s + pltpu.roll(s, sh, 0)
            sh *= 2
        outs.append(rows(s) / cnt[gi] - rows(a))
    return outs


def _pool_project(d_list, wp_ref, ps):
    y = jnp.concatenate([_dot(d.astype(MXU_DTYPE), wp_ref[gi]) for gi, d in enumerate(d_list)], axis=1)
    return y * ps


def _inproj_prompt_kernel(x_ref, g_ref, w_ref, rc_ref, rs1_ref, rs2_ref, cp_ref, wc_ref, cc_ref, cs1_ref, cs2_ref,
                          wp_ref, ps_ref,
                          q_ref, kvc_ref, kvs_ref, kvw_ref, kvsb_ref, kvwb_ref, gates_ref, pool_ref, ulast_ref,
                          ckv_ref, z_scr):
    j = pl.program_id(1)
    tm = x_ref.shape[0]
    nblk = tm // L_CMP
    r0 = pl.multiple_of(j * tm, tm)
    rows = pl.ds(r0, tm)
    q, kvc, kvs, kvw, u, gates = _project_rows(x_ref[...], g_ref[...], w_ref, rc_ref[rows, :], rs1_ref[rows, :],
                                               rs2_ref[rows, :])
    q_ref[...] = q.astype(q_ref.dtype)
    kvc_ref[0] = kvc.T
    kvs_ref[0] = kvs.T
    kvw_ref[0] = kvw.T
    kvsb_ref[...] = kvs.astype(kvsb_ref.dtype)
    kvwb_ref[...] = kvw.astype(kvwb_ref.dtype)
    gates_ref[...] = gates

    crow = pl.ds(pl.multiple_of(j * nblk, nblk), nblk)
    ckv_ref[...] = _compress_rows(kvc, cp_ref[...], wc_ref, cc_ref[crow, :], cs1_ref[crow, :], cs2_ref[crow, :])

    @pl.when(j == 0)
    def _():
        z_scr[0:HIST_ROWS, :] = jnp.zeros((HIST_ROWS, POOL_W), F32)

    @pl.when(j > 0)
    def _():
        z_scr[0:HIST_ROWS, :] = z_scr[tm:tm + HIST_ROWS, :]

    z_scr[HIST_ROWS:, :] = u
    pos = (r0 + lax.broadcasted_iota(jnp.int32, (tm, 1), 0)).astype(F32)
    cnt = [jnp.minimum(pos + 1.0, float(w)) for w in POOL_WINDOWS]
    d_list = _pool_windows(z_scr[...], lambda a: a[HIST_ROWS:, :], cnt)
    pool_ref[...] = _pool_project(d_list, wp_ref, ps_ref[...]).astype(pool_ref.dtype)
    ulast_ref[0] = u[tm - HIST_ROWS:, :]


def _inproj_prompt(x2, g_pre, wts, seq, batch):
    n = x2.shape[0]
    tm = min(TM_IN, seq)
    nj = seq // tm
    nblk = tm // L_CMP
    rc, rs1, rs2 = _rope_tables(np.arange(seq))
    cc, cs1, cs2 = _rope_tables((np.arange(seq // L_CMP) + 1) * L_CMP - 1)
    row = lambda w: pl.BlockSpec((tm, w), lambda b, j: (b * nj + j, 0))
    out_shape = (
        jax.ShapeDtypeStruct((n, ATT_W), MXU_DTYPE),
        jax.ShapeDtypeStruct((batch, 2 * KV_W, seq), F32),
        jax.ShapeDtypeStruct((batch, 2 * KV_W, seq), F32),
        jax.ShapeDtypeStruct((batch, 2 * KV_W, seq), F32),
        jax.ShapeDtypeStruct((n, 2 * KV_W), MXU_DTYPE),
        jax.ShapeDtypeStruct((n, 2 * KV_W), MXU_DTYPE),
        jax.ShapeDtypeStruct((n, LANES), F32),
        jax.ShapeDtypeStruct((n, POOL_W), MXU_DTYPE),
        jax.ShapeDtypeStruct((batch, HIST_ROWS, POOL_W), F32),
        jax.ShapeDtypeStruct((n // L_CMP, 2 * KV_W), F32),
    )
    dims_major = pl.BlockSpec((1, 2 * KV_W, tm), lambda b, j: (b, 0, j))
    out_specs = (row(ATT_W), dims_major, dims_major, dims_major, row(2 * KV_W), row(2 * KV_W), row(LANES),
                 row(POOL_W), pl.BlockSpec((1, HIST_ROWS, POOL_W), lambda b, j: (b, 0, 0)),
                 pl.BlockSpec((nblk, 2 * KV_W), lambda b, j: (b * nj + j, 0)))
    in_specs = [row(D_MODEL), _const_spec((1, D_MODEL)), _const_spec((D_MODEL, IN_W_PAD)),
                _const_spec((seq, LANES)), _const_spec((seq, LANES)), _const_spec((seq, LANES)),
                _const_spec((L_CMP, 2 * KV_W)), _const_spec((2 * KV_W, 2 * KV_W)),
                _const_spec((seq // L_CMP, LANES)), _const_spec((seq // L_CMP, LANES)),
                _const_spec((seq // L_CMP, LANES)),
                _const_spec((len(POOL_WINDOWS), POOL_GC, POOL_GC)), _const_spec((1, POOL_W))]
    return pl.pallas_call(
        _inproj_prompt_kernel,
        grid=(batch, nj),
        in_specs=in_specs, out_specs=out_specs, out_shape=out_shape,
        scratch_shapes=[pltpu.VMEM((tm + HIST_ROWS, POOL_W), F32)],
        compiler_params=pltpu.CompilerParams(dimension_semantics=("arbitrary", "arbitrary"),
                                             vmem_limit_bytes=VMEM_LIMIT),
    )(x2, g_pre.reshape(1, D_MODEL), wts["w_in"], rc, rs1, rs2, wts["cp"], wts["wc"], cc, cs1, cs2,
      wts["w_pool"], wts["pool_scale"])


def _post_kernel(x_ref, o_ref, p_ref, wo_ref, gpm_ref, gpf_ref, gqf_ref, wg_ref, wu_ref, wd_ref, y_ref):
    mixed = _dot(jnp.concatenate([o_ref[...].astype(MXU_DTYPE), p_ref[...]], axis=1), wo_ref[...])
    x1 = x_ref[...] + _rms(mixed, gpm_ref[...])
    h2 = _rms(x1, gpf_ref[...]).astype(MXU_DTYPE)
    acc = None
    for c in range(FF_CHUNKS):
        f = (jax.nn.silu(_dot(h2, wg_ref[c])) * _dot(h2, wu_ref[c])).astype(MXU_DTYPE)
        part = _dot(f, wd_ref[c])
        acc = part if acc is None else acc + part
    y_ref[...] = x1 + _rms(acc, gqf_ref[...])


def _post(x2, o_att, pool_y, g_post_mix, g_pre_ffn, g_post_ffn, wts):
    n = x2.shape[0]
    tm = min(TM_POST, n)
    fc = D_FF // FF_CHUNKS
    row = lambda w: pl.BlockSpec((tm, w), lambda i: (i, 0))
    return pl.pallas_call(
        _post_kernel,
        grid=(n // tm,),
        in_specs=[row(D_MODEL), row(ATT_W), row(POOL_W), _const_spec((D_MODEL, D_MODEL)),
                  _const_spec((1, D_MODEL)), _const_spec((1, D_MODEL)), _const_spec((1, D_MODEL)),
                  _const_spec((FF_CHUNKS, D_MODEL, fc)), _const_spec((FF_CHUNKS, D_MODEL, fc)),
                  _const_spec((FF_CHUNKS, fc, D_MODEL))],
        out_specs=row(D_MODEL),
        out_shape=jax.ShapeDtypeStruct((n, D_MODEL), F32),
        compiler_params=pltpu.CompilerParams(dimension_semantics=("arbitrary",), vmem_limit_bytes=VMEM_LIMIT),
    )(x2, o_att, pool_y, wts["w_out"], g_post_mix.reshape(1, D_MODEL), g_pre_ffn.reshape(1, D_MODEL),
      g_post_ffn.reshape(1, D_MODEL), wts["w_gate"], wts["w_up"], wts["w_down"])


def _select_blocks_t(p_slc_t, c_row, n_top):
    nb = p_slc_t.shape[0]
    jj = lax.broadcasted_iota(jnp.int32, p_slc_t.shape, 0)
    cand = (jj >= 1) & (jj <= c_row - 2)
    sc = jnp.where(cand, p_slc_t, -1.0)
    rank = jnp.zeros(sc.shape, F32)
    for i in range(nb):
        row = sc[i:i + 1, :]
        beats = (row > sc) | ((row == sc) & (jj > i))
        rank = rank + jnp.where(beats, 1.0, 0.0)
    return (cand & (rank < n_top)) | (jj == 0) | (jj == c_row) | (jj == c_row - 1)


def _lane_tiles(a, n):
    return jnp.concatenate([a] * n, axis=1)


def _attn_prompt_kernel(q_ref, gt_ref, ckv_ref, ks_ref, kw_ref, eneg_ref, o_ref,
                        qaug_scr, oc_scr, s_scr, mel_scr, mb_scr, lel_scr, acc_scr):
    qb = pl.program_id(1)
    qblk = q_ref.shape[0]
    rows = HPG * qblk
    ncb = ckv_ref.shape[1]
    half = ncb // 2
    wspan = WINDOW + qblk
    start = qb * qblk
    tpos_c = start + lax.broadcasted_iota(jnp.int32, (qblk, 1), 0)
    tpos_r = start + lax.broadcasted_iota(jnp.int32, (1, qblk), 1)
    gs = jax.nn.sigmoid(gt_ref[...])
    ckv = ckv_ref[0]
    ck = ckv[:, :KV_W].astype(MXU_DTYPE)
    cv = ckv[:, KV_W:].astype(MXU_DTYPE)
    crow = lax.broadcasted_iota(jnp.int32, (ncb, rows), 0)
    cblk = 2 * (crow % half) + crow // half
    cmask = ((cblk + 1) * L_CMP - 1) <= _lane_tiles(tpos_r, HPG)

    ws = pl.multiple_of(jnp.maximum(start - WINDOW, 0), qblk)
    dwin = tpos_c - (ws + lax.broadcasted_iota(jnp.int32, (1, wspan), 1))
    bias_w = jnp.where((dwin >= 0) & (dwin < WINDOW), 0.0, NEG)
    n_chunks = start // K_CHUNK + 1
    lane = lax.broadcasted_iota(jnp.int32, (qblk, LANES), 1)

    outs = [None] * N_HEADS
    for g in range(KV_HEADS):
        pieces = []
        for h in range(HPG):
            hh = g * HPG + h
            tile = q_ref[:, (hh // 2) * LANES:(hh // 2 + 1) * LANES].astype(F32)
            if hh % 2 != g:
                tile = pltpu.roll(tile, HEAD_DIM, 1)
            pieces.append(jnp.where((lane >= g * HEAD_DIM) & (lane < (g + 1) * HEAD_DIM), tile, 0.0))
        qg = jnp.concatenate(pieces, axis=0).astype(MXU_DTYPE)

        s = jnp.where(cmask, _dot_nt(ck, qg), NEG)
        e = jnp.where(cmask, jnp.exp(s - jnp.max(s, axis=0, keepdims=True)), 0.0)
        p = e / jnp.maximum(jnp.sum(e, axis=0, keepdims=True), 1e-30)
        o_c = _dot_tn(p.astype(MXU_DTYPE), cv)
        pc = p[:, 0:qblk]
        for h in range(1, HPG):
            pc = pc + p[:, h * qblk:(h + 1) * qblk]
        sel = _select_blocks_t(pc[0:half] + pc[half:ncb], tpos_r // L_SEL, N_TOP)
        unsel = jnp.concatenate([jnp.where(sel, 0.0, 1.0), jnp.zeros((LANES - half, qblk), F32)], axis=0)
        unsel = unsel.T.astype(MXU_DTYPE)
        gr = slice(g * rows, (g + 1) * rows)
        qaug_scr[gr, :] = jnp.concatenate([qg, jnp.concatenate([unsel] * HPG, axis=0)], axis=1)
        oc_scr[gr, :] = o_c

    mel_scr[...] = jnp.full((KV_HEADS * rows, LANES), NEG, F32)

    def score_chunk(ci, causal):
        k0 = pl.multiple_of(ci * K_CHUNK, K_CHUNK)
        kaug = jnp.concatenate([ks_ref[0, pl.ds(k0, K_CHUNK), 0:KV_W], eneg_ref[pl.ds(k0, K_CHUNK), :]], axis=1)
        sv = _dot_nt(qaug_scr[...], kaug)
        if causal:
            kpos = k0 + lax.broadcasted_iota(jnp.int32, (1, K_CHUNK), 1)
            bias = jnp.where(kpos <= tpos_c, 0.0, NEG)
            sv = jnp.concatenate([sv[r * qblk:(r + 1) * qblk] + bias for r in range(KV_HEADS * HPG)], axis=0)
        s_scr[ci] = sv
        mh = sv[:, 0:LANES]
        for j in range(1, K_CHUNK // LANES):
            mh = jnp.maximum(mh, sv[:, j * LANES:(j + 1) * LANES])
        mel_scr[...] = jnp.maximum(mel_scr[...], mh)

    def score_body(ci, _):
        score_chunk(ci, False)
        return 0

    lax.fori_loop(0, n_chunks - 1, score_body, 0)
    score_chunk(n_chunks - 1, True)
    mb_scr[...] = jnp.broadcast_to(jnp.max(mel_scr[...], axis=1, keepdims=True), (KV_HEADS * rows, LANES))
    lel_scr[...] = jnp.zeros((KV_HEADS * rows, LANES), F32)
    acc_scr[...] = jnp.zeros((KV_HEADS * rows, LANES), F32)

    def value_body(ci, _):
        k0 = pl.multiple_of(ci * K_CHUNK, K_CHUNK)
        pv = jnp.exp(s_scr[ci] - _lane_tiles(mb_scr[...], K_CHUNK // LANES))
        lsum = pv[:, 0:LANES]
        for j in range(1, K_CHUNK // LANES):
            lsum = lsum + pv[:, j * LANES:(j + 1) * LANES]
        lel_scr[...] += lsum
        acc_scr[...] += _dot(pv.astype(MXU_DTYPE), ks_ref[0, pl.ds(k0, K_CHUNK), KV_W:2 * KV_W])
        return 0

    lax.fori_loop(0, n_chunks, value_body, 0)
    acc_scr[...] = acc_scr[...] / jnp.maximum(jnp.sum(lel_scr[...], axis=1, keepdims=True), 1e-30)

    for g in range(KV_HEADS):
        gr = slice(g * rows, (g + 1) * rows)
        o_c = oc_scr[gr, :]
        o_s = acc_scr[gr, :]
        qg = qaug_scr[gr, 0:KV_W]
        sw = _dot_nt(qg, kw_ref[0, pl.ds(ws, wspan), 0:KV_W])
        sw = jnp.concatenate([sw[h * qblk:(h + 1) * qblk] + bias_w for h in range(HPG)], axis=0)
        ew = jnp.exp(sw - jnp.max(sw, axis=1, keepdims=True))
        o_w = (_dot(ew.astype(MXU_DTYPE), kw_ref[0, pl.ds(ws, wspan), KV_W:2 * KV_W])
               / jnp.maximum(jnp.sum(ew, axis=1, keepdims=True), 1e-30))

        for h in range(HPG):
            hh = g * HPG + h
            col = hh * N_BRANCH
            rs = slice(h * qblk, (h + 1) * qblk)
            cs = slice(g * HEAD_DIM, (g + 1) * HEAD_DIM)
            outs[hh] = (gs[:, col:col + 1] * o_c[rs, cs] + gs[:, col + 1:col + 2] * o_s[rs, cs]
                        + gs[:, col + 2:col + 3] * o_w[rs, cs])
    o_ref[...] = jnp.concatenate(outs, axis=1).astype(o_ref.dtype)


def _attn_prompt(q, gates, ckv_perm, kvsb, kvwb, seq, batch):
    n = q.shape[0]
    qblk = min(Q_BLK, seq)
    nqb = seq // qblk
    nb = seq // L_SEL
    nchunk = seq // K_CHUNK
    rows = HPG * qblk
    assert seq >= WINDOW + qblk and seq % K_CHUNK == 0 and K_CHUNK % qblk == 0
    assert nb <= LANES
    eneg = jnp.asarray(NEG * (np.arange(seq)[:, None] // L_SEL == np.arange(LANES)[None, :]), MXU_DTYPE)
    row = lambda w: pl.BlockSpec((qblk, w), lambda b, i: (b * nqb + i, 0))
    per_b = lambda r, w: pl.BlockSpec((1, r, w), lambda b, i: (b, 0, 0))
    return pl.pallas_call(
        _attn_prompt_kernel,
        grid=(batch, nqb),
        in_specs=[row(ATT_W), row(LANES), per_b(seq // L_CMP, 2 * KV_W), per_b(seq, 2 * KV_W),
                  per_b(seq, 2 * KV_W), _const_spec((seq, LANES))],
        out_specs=row(ATT_W),
        out_shape=jax.ShapeDtypeStruct((n, ATT_W), MXU_DTYPE),
        scratch_shapes=[pltpu.VMEM((KV_HEADS * rows, 2 * LANES), MXU_DTYPE),
                        pltpu.VMEM((KV_HEADS * rows, LANES), F32),
                        pltpu.VMEM((nchunk, KV_HEADS * rows, K_CHUNK), F32)]
        + [pltpu.VMEM((KV_HEADS * rows, LANES), F32)] * 4,
        compiler_params=pltpu.CompilerParams(dimension_semantics=("arbitrary", "arbitrary"),
                                             vmem_limit_bytes=VMEM_LIMIT),
    )(q, gates, ckv_perm, kvsb.reshape(batch, seq, 2 * KV_W), kvwb.reshape(batch, seq, 2 * KV_W), eneg)


def _prompt_group(x_prompt, g_pre_mix, g_post_mix, g_pre_ffn, g_post_ffn, wts):
    batch, seq, _ = x_prompt.shape
    x2 = x_prompt.reshape(batch * seq, D_MODEL)
    q, kvc, kvs, kvw, kvsb, kvwb, gates, pool_y, ulast, ckv = _inproj_prompt(x2, g_pre_mix, wts, seq, batch)
    ncb = seq // L_CMP
    ckv_perm = ckv.reshape(batch, ncb // 2, 2, 2 * KV_W).transpose(0, 2, 1, 3).reshape(batch, ncb, 2 * KV_W)
    o_att = _attn_prompt(q, gates, ckv_perm, kvsb, kvwb, seq, batch)
    y = _post(x2, o_att, pool_y, g_post_mix, g_pre_ffn, g_post_ffn, wts)
    wkeep = min(WINDOW, seq)
    kv6 = lambda a: jnp.transpose(a.reshape(1, batch, 2, KV_HEADS, HEAD_DIM, a.shape[-1]), (0, 1, 5, 2, 3, 4))
    return (y.reshape(batch, seq, D_MODEL), kv6(kvc), kv6(kvs), kv6(kvw[:, :, seq - wkeep:]),
            ulast[None, :, 1:, :])


def _inproj_sample_kernel(x_ref, g_ref, w_ref, rc_ref, rs1_ref, rs2_ref, cp_ref, wc_ref, cc_ref, cs1_ref, cs2_ref,
                          wp_ref, ps_ref, hist_ref,
                          q_ref, kvc_ref, kvs_ref, kvw_ref, gates_ref, pool_ref, znew_ref, ckv_ref, z_scr, *, past):
    nseq, zrows, _ = z_scr.shape
    t_new = zrows - HIST_ROWS
    n = nseq * t_new
    q, kvc, kvs, kvw, u, gates = _project_rows(x_ref[...], g_ref[...], w_ref, rc_ref[...], rs1_ref[...],
                                               rs2_ref[...])
    q_ref[...] = q
    kvc_ref[...] = kvc
    kvs_ref[...] = kvs
    kvw_ref[...] = kvw
    gates_ref[...] = gates

    summ = jnp.sum(kvc.reshape(nseq, t_new, 2 * KV_W) * cp_ref[0:t_new, :][None], axis=1)
    ckv = _dot(summ.astype(MXU_DTYPE), wc_ref[...])
    ckv_ref[...] = jnp.concatenate([_rope128(ckv[:, :KV_W], cc_ref[...], cs1_ref[...], cs2_ref[...]),
                                    ckv[:, KV_W:]], axis=1)

    z_scr[:, 0:HIST_ROWS, :] = hist_ref[...]
    z_scr[:, HIST_ROWS:, :] = u.reshape(nseq, t_new, POOL_W)
    zf = z_scr[...].reshape(nseq * zrows, POOL_W)
    tok = lax.broadcasted_iota(jnp.int32, (n, 1), 0) % t_new
    pos = (past + tok).astype(F32)
    cnt = [jnp.minimum(pos + 1.0, float(w)) for w in POOL_WINDOWS]
    take = lambda a: a.reshape(nseq, zrows, POOL_GC)[:, HIST_ROWS:, :].reshape(n, POOL_GC)
    d_list = _pool_windows(zf, take, cnt)
    pool_ref[...] = _pool_project(d_list, wp_ref, ps_ref[...]).astype(pool_ref.dtype)
    znew_ref[...] = z_scr[:, zrows - HIST_ROWS:, :]


def _inproj_sample(x2, g_pre, wts, hist16, nseq, t_new, past):
    n = x2.shape[0]
    rc, rs1, rs2 = _rope_tables(np.tile(past + np.arange(t_new), nseq))
    cc, cs1, cs2 = _rope_tables(np.array([past + L_CMP - 1]))
    full = lambda *s: pl.BlockSpec(s, lambda i: (0,) * len(s))
    out_shape = (
        jax.ShapeDtypeStruct((n, ATT_W), F32),
        jax.ShapeDtypeStruct((n, 2 * KV_W), F32),
        jax.ShapeDtypeStruct((n, 2 * KV_W), F32),
        jax.ShapeDtypeStruct((n, 2 * KV_W), F32),
        jax.ShapeDtypeStruct((n, LANES), F32),
        jax.ShapeDtypeStruct((n, POOL_W), MXU_DTYPE),
        jax.ShapeDtypeStruct((nseq, HIST_ROWS, POOL_W), F32),
        jax.ShapeDtypeStruct((nseq, 2 * KV_W), F32),
    )
    out_specs = (full(n, ATT_W), full(n, 2 * KV_W), full(n, 2 * KV_W), full(n, 2 * KV_W), full(n, LANES),
                 full(n, POOL_W), full(nseq, HIST_ROWS, POOL_W), full(nseq, 2 * KV_W))
    in_specs = [full(n, D_MODEL), full(1, D_MODEL), full(D_MODEL, IN_W_PAD), full(n, LANES), full(n, LANES),
                full(n, LANES), full(L_CMP, 2 * KV_W), full(2 * KV_W, 2 * KV_W), full(1, LANES), full(1, LANES),
                full(1, LANES), full(len(POOL_WINDOWS), POOL_GC, POOL_GC), full(1, POOL_W),
                full(nseq, HIST_ROWS, POOL_W)]
    return pl.pallas_call(
        functools.partial(_inproj_sample_kernel, past=past),
        grid=(1,),
        in_specs=in_specs, out_specs=out_specs, out_shape=out_shape,
        scratch_shapes=[pltpu.VMEM((nseq, HIST_ROWS + t_new, POOL_W), F32)],
        compiler_params=pltpu.CompilerParams(dimension_semantics=("arbitrary",), vmem_limit_bytes=VMEM_LIMIT),
    )(x2, g_pre.reshape(1, D_MODEL), wts["w_in"], rc, rs1, rs2, wts["cp"], wts["wc"], cc, cs1, cs2,
      wts["w_pool"], wts["pool_scale"], hist16)


PAGE_CHUNK = 8
CMP_LANE_BLKS = PAGE_SIZE // L_CMP
HALF_PAGES = LANES // CMP_LANE_BLKS
CMP_PAGES_PER_STEP = 4


def _sample_tables(past, t_new, wb):
    npages = past // PAGE_SIZE
    nrow = HPG * KV_HEADS * t_new
    r = np.arange(nrow)
    t_row = r % t_new
    pos = past + t_row
    lanes = np.arange(npages * CMP_LANE_BLKS)
    page = HALF_PAGES * (lanes // LANES) + lanes % HALF_PAGES
    m = (lanes % LANES) // HALF_PAGES
    blk = CMP_LANE_BLKS * page + (m - 1) % CMP_LANE_BLKS
    cmask = (blk[None, :] + 1) * L_CMP - 1 <= pos[:, None]
    n_new = 2 * t_new
    blk_new = past // L_CMP + np.arange(n_new)
    n_cmp = (past + -(-t_new // L_SEL) * L_SEL) // L_CMP
    cmask_new = ((blk_new[None, :] + 1) * L_CMP - 1 <= pos[:, None]) & (blk_new[None, :] < n_cmp)
    jidx = np.where(m == 1, 2 * page, np.where(m == 3, 2 * page + 1, -1))
    rg = np.arange(KV_HEADS * t_new)
    c = ((past + rg % t_new) // L_SEL)[:, None]
    jj = jidx[None, :]
    cand = (jj >= 1) & (jj <= c - 2)
    forced = (jj >= 0) & ((jj == 0) | (jj == c) | (jj == c - 1))
    l = np.arange(n_new)
    new_blk = past // L_SEL
    c_row = (pos // L_SEL)[:, None]
    new_forced = (new_blk == 0) | (new_blk == c_row) | (new_blk == c_row - 1)
    nmask = new_forced & (l[None, :] <= t_row[:, None]) & (l[None, :] < t_new)
    i = np.arange(wb)
    d = t_row[:, None] + wb - i[None, :]
    wmask = (d >= 0) & (d < WINDOW) & (past - wb + i[None, :] >= 0)
    dn = t_row[:, None] - l[None, :]
    wmask_new = (dn >= 0) & (dn < WINDOW) & (l[None, :] < t_new)
    ck = PAGE_CHUNK * PAGE_SIZE
    k = np.arange(ck)
    eeo = np.zeros((npages // PAGE_CHUNK, 2 * npages, ck), np.float32)
    for ch in range(npages // PAGE_CHUNK):
        pg = PAGE_CHUNK * ch + k // PAGE_SIZE
        upper = (k % PAGE_SIZE) >= L_SEL
        eeo[ch, pg[~upper], k[~upper]] = 1.0
        eeo[ch, npages + pg[upper], k[upper]] = 1.0
    tt = np.arange(PAGE_SIZE)
    steps = HALF_PAGES // CMP_PAGES_PER_STEP
    gsum = np.zeros((steps, CMP_PAGES_PER_STEP * PAGE_SIZE, LANES), np.float32)
    for it in range(steps):
        for kk in range(CMP_PAGES_PER_STEP):
            col = HALF_PAGES * ((tt // L_CMP + 1) % CMP_LANE_BLKS) + it * CMP_PAGES_PER_STEP + kk
            gsum[it, kk * PAGE_SIZE + tt, col] = 1.0
    f = lambda a: jnp.asarray(np.asarray(a, np.float32))
    return dict(gsum=jnp.asarray(gsum, MXU_DTYPE), cmask=f(cmask), cmask_new=f(cmask_new), cand=f(cand), forced=f(forced), nmask=f(nmask),
                wmask=f(wmask), wmask_new=f(wmask_new), jidx=jnp.asarray(jidx[None, :], jnp.int32),
                eeo=jnp.asarray(eeo, MXU_DTYPE), blk=blk, jidx_np=jidx)


def _rope_tables_t(pos):
    c, s1, s2 = _rope_tables(pos)
    return c.T, s1.T, s2.T


def _attn_sample_kernel(pt_ref, q_ref, gt_ref, kvs_ref, kvw_ref, ckv0_ref, cw_ref, cmp_hbm, sel_hbm,
                        cpt_ref, wct_ref, cc_ref, cs1_ref, cs2_ref, cmask_ref, cmaskn_ref, cand_ref, forced_ref,
                        nmask_ref, wmask_ref, wmaskn_ref, jidx_ref, eeo_ref, gsum_ref,
                        o_ref,
                        cbuf, sbuf, sem, summ_scr, s_scr, *, jidx_np, n_top):
    b = pl.program_id(0)
    nseq = pl.num_programs(0)
    npages = pt_ref.shape[1]
    t_new = q_ref.shape[1]
    nrow = HPG * KV_HEADS * t_new
    ngt = KV_HEADS * t_new
    slot = b % 2

    def copies(seq, sl):
        out = []
        for p in range(npages):
            pg = pt_ref[seq, p]
            out.append(pltpu.make_async_copy(cmp_hbm.at[pg], cbuf.at[sl, p], sem.at[0, sl]))
            out.append(pltpu.make_async_copy(sel_hbm.at[pg], sbuf.at[sl, p], sem.at[1, sl]))
        return out

    @pl.when(b == 0)
    def _():
        for cpy in copies(b, slot):
            cpy.start()

    @pl.when(b + 1 < nseq)
    def _():
        for cpy in copies(b + 1, 1 - slot):
            cpy.start()

    qb = q_ref[0]
    zeros = jnp.zeros((t_new, HEAD_DIM), F32)
    blocks = []
    for h in range(HPG):
        for g in range(KV_HEADS):
            piece = qb[:, (g * HPG + h) * HEAD_DIM:(g * HPG + h + 1) * HEAD_DIM]
            blocks.append(jnp.concatenate([piece, zeros] if g == 0 else [zeros, piece], axis=1))
    qbd = jnp.concatenate(blocks, axis=0).astype(MXU_DTYPE)

    def new_rows(rows):
        a = jnp.concatenate([rows, jnp.zeros((2 * t_new - rows.shape[0], 2 * KV_W), F32)], axis=0)
        return a[:, :KV_W].astype(MXU_DTYPE), a[:, KV_W:].astype(MXU_DTYPE)

    def joint_softmax(s1, ok1, s2, ok2):
        s1 = jnp.where(ok1, s1, NEG)
        s2 = jnp.where(ok2, s2, NEG)
        m = jnp.maximum(jnp.max(s1, axis=1, keepdims=True), jnp.max(s2, axis=1, keepdims=True))
        e1 = jnp.where(ok1, jnp.exp(s1 - m), 0.0)
        e2 = jnp.where(ok2, jnp.exp(s2 - m), 0.0)
        l = jnp.maximum(jnp.sum(e1, axis=1, keepdims=True) + jnp.sum(e2, axis=1, keepdims=True), 1e-30)
        return e1 / l, e2 / l

    kwn, vwn = new_rows(kvw_ref[0])
    p1, p2 = joint_softmax(_dot(qbd, cw_ref[0, 0:KV_W, :].astype(MXU_DTYPE)), wmask_ref[...] > 0.5,
                           _dot_nt(qbd, kwn), wmaskn_ref[...] > 0.5)
    o_w = (_dot_nt(p1.astype(MXU_DTYPE), cw_ref[0, KV_W:2 * KV_W, :].astype(MXU_DTYPE))
           + _dot(p2.astype(MXU_DTYPE), vwn))

    for cpy in copies(b, slot):
        cpy.wait()

    nhalf = npages // HALF_PAGES
    summ_scr[...] = jnp.zeros(summ_scr.shape, F32)

    def cmp_body(it, _):
        xs = []
        for k in range(CMP_PAGES_PER_STEP):
            pg = it * CMP_PAGES_PER_STEP + k
            x = jnp.concatenate([cbuf[slot, h2 * HALF_PAGES + pg] * cpt_ref[...] for h2 in range(nhalf)], axis=0)
            xs.append(x.astype(MXU_DTYPE))
        summ_scr[...] += _dot(jnp.concatenate(xs, axis=1), gsum_ref[it])
        return 0

    lax.fori_loop(0, HALF_PAGES // CMP_PAGES_PER_STEP, cmp_body, 0)
    summ = jnp.concatenate([summ_scr[h2 * 2 * KV_W:(h2 + 1) * 2 * KV_W] for h2 in range(nhalf)], axis=1)
    ckv = _dot(wct_ref[...], summ.astype(MXU_DTYPE))
    kc = ckv[0:KV_W]
    ckt = (kc * cc_ref[...] + pltpu.roll(kc, ROT_HALF, 0) * cs1_ref[...]
           + pltpu.roll(kc, KV_W - ROT_HALF, 0) * cs2_ref[...]).astype(MXU_DTYPE)
    cvt = ckv[KV_W:2 * KV_W].astype(MXU_DTYPE)

    ckn, cvn = new_rows(ckv0_ref[0])
    p1, p2 = joint_softmax(_dot(qbd, ckt), cmask_ref[...] > 0.5, _dot_nt(qbd, ckn), cmaskn_ref[...] > 0.5)
    o_c = _dot_nt(p1.astype(MXU_DTYPE), cvt) + _dot(p2.astype(MXU_DTYPE), cvn)
    pc = p1[0:ngt]
    for h in range(1, HPG):
        pc = pc + p1[h * ngt:(h + 1) * ngt]
    pair = jnp.concatenate(
        [pc[:, t * LANES:(t + 1) * LANES] + pltpu.roll(pc[:, t * LANES:(t + 1) * LANES], LANES - HALF_PAGES, 1)
         for t in range(pc.shape[1] // LANES)], axis=1)

    cand = cand_ref[...] > 0.5
    sc = jnp.where(cand, pair, -1.0)
    jl = jidx_ref[...]
    rank = jnp.zeros(sc.shape, F32)
    for i in np.nonzero(jidx_np >= 0)[0]:
        col = sc[:, int(i):int(i) + 1]
        beats = (col > sc) | ((col == sc) & (jl > int(jidx_np[i])))
        rank = rank + jnp.where(beats, 1.0, 0.0)
    sel = jnp.where((cand & (rank < n_top)) | (forced_ref[...] > 0.5), 1.0, 0.0)
    sel_eo = jnp.concatenate(
        [sel[:, t * LANES + HALF_PAGES:t * LANES + 2 * HALF_PAGES] for t in range(sel.shape[1] // LANES)]
        + [sel[:, t * LANES + 3 * HALF_PAGES:(t + 1) * LANES] for t in range(sel.shape[1] // LANES)],
        axis=1).astype(MXU_DTYPE)

    def score_body(ch, mel):
        keep = _dot(sel_eo, eeo_ref[ch]) > 0.5
        bias = jnp.where(keep, 0.0, NEG)
        bias = jnp.concatenate([bias] * HPG, axis=0)
        for pp in range(0, PAGE_CHUNK, 2):
            pg = ch * PAGE_CHUNK + pp
            kt = jnp.concatenate([sbuf[slot, pg, 0:KV_W, :], sbuf[slot, pg + 1, 0:KV_W, :]], axis=1)
            sv = _dot(qbd, kt.astype(MXU_DTYPE)) + bias[:, pp * LANES:(pp + 2) * LANES]
            s_scr[pg] = sv[:, 0:LANES]
            s_scr[pg + 1] = sv[:, LANES:2 * LANES]
            mel = jnp.maximum(mel, jnp.maximum(sv[:, 0:LANES], sv[:, LANES:2 * LANES]))
        return mel

    mel = lax.fori_loop(0, npages // PAGE_CHUNK, score_body, jnp.full((nrow, LANES), NEG, F32))
    ksn, vsn = new_rows(kvs_ref[0])
    nok = nmask_ref[...] > 0.5
    s_new = jnp.where(nok, _dot_nt(qbd, ksn), NEG)
    m = jnp.maximum(jnp.max(mel, axis=1, keepdims=True), jnp.max(s_new, axis=1, keepdims=True))
    mb = jnp.broadcast_to(m, (nrow, LANES))

    def value_body(ch, carry):
        lel, acc = carry
        for pp in range(0, PAGE_CHUNK, 2):
            pg = ch * PAGE_CHUNK + pp
            pv0 = jnp.exp(s_scr[pg] - mb)
            pv1 = jnp.exp(s_scr[pg + 1] - mb)
            lel = lel + (pv0 + pv1)
            vt = jnp.concatenate([sbuf[slot, pg, KV_W:2 * KV_W, :], sbuf[slot, pg + 1, KV_W:2 * KV_W, :]], axis=1)
            acc = acc + _dot_nt(jnp.concatenate([pv0, pv1], axis=1).astype(MXU_DTYPE), vt.astype(MXU_DTYPE))
        return lel, acc

    e_new = jnp.where(nok, jnp.exp(s_new - m), 0.0)
    lel, acc = lax.fori_loop(0, npages // PAGE_CHUNK, value_body,
                             (jnp.zeros((nrow, LANES), F32), _dot(e_new.astype(MXU_DTYPE), vsn)))
    l = jnp.sum(lel, axis=1, keepdims=True) + jnp.sum(e_new, axis=1, keepdims=True)
    o_s = acc / jnp.maximum(l, 1e-30)

    gs = jax.nn.sigmoid(gt_ref[0])
    outs = [None] * N_HEADS
    for h in range(HPG):
        for g in range(KV_HEADS):
            hh = g * HPG + h
            rs = slice((h * KV_HEADS + g) * t_new, (h * KV_HEADS + g + 1) * t_new)
            cs = slice(g * HEAD_DIM, (g + 1) * HEAD_DIM)
            col = hh * N_BRANCH
            outs[hh] = (gs[:, col:col + 1] * o_c[rs, cs] + gs[:, col + 1:col + 2] * o_s[rs, cs]
                        + gs[:, col + 2:col + 3] * o_w[rs, cs])
    o_ref[0] = jnp.concatenate(outs, axis=1)


def _dims_major(a):
    n, toks = a.shape[:2]
    return jnp.transpose(a, (0, 2, 3, 4, 1)).reshape(n, 2 * KV_W, toks)


def _attn_sample(q, gates, kvs, kvw, ckv0, cache_cmp, cache_sel, cache_win, page_table, wts, t_new):
    nseq, npages = page_table.shape
    past = npages * PAGE_SIZE
    wb = cache_win.shape[1]
    nrow = HPG * KV_HEADS * t_new
    ngt = KV_HEADS * t_new
    nlane = npages * CMP_LANE_BLKS
    assert t_new <= L_CMP and past % L_SEL == 0 and wb + t_new >= WINDOW
    assert npages % HALF_PAGES == 0 and npages % PAGE_CHUNK == 0 and 2 * npages == LANES
    tb = _sample_tables(past, t_new, wb)
    cc, cs1, cs2 = _rope_tables_t((tb["blk"] + 1) * L_CMP - 1)
    cpt = jnp.tile(wts["cp"].T, (1, CMP_LANE_BLKS))
    seq3 = lambda r, w: pl.BlockSpec((1, r, w), lambda b, pt: (b, 0, 0))
    const = lambda *s: pl.BlockSpec(s, lambda b, pt: (0,) * len(s), pipeline_mode=pl.Buffered(1))
    hbm = pl.BlockSpec(memory_space=pl.ANY)
    in_specs = [seq3(t_new, ATT_W), seq3(t_new, LANES), seq3(t_new, 2 * KV_W), seq3(t_new, 2 * KV_W),
                seq3(1, 2 * KV_W), seq3(2 * KV_W, wb), hbm, hbm,
                const(2 * KV_W, PAGE_SIZE), const(2 * KV_W, 2 * KV_W), const(KV_W, nlane), const(KV_W, nlane),
                const(KV_W, nlane), const(nrow, nlane), const(nrow, 2 * t_new), const(ngt, nlane),
                const(ngt, nlane), const(nrow, 2 * t_new), const(nrow, wb), const(nrow, 2 * t_new),
                const(1, nlane), const(npages // PAGE_CHUNK, 2 * npages, PAGE_CHUNK * PAGE_SIZE),
                const(HALF_PAGES // CMP_PAGES_PER_STEP, CMP_PAGES_PER_STEP * PAGE_SIZE, LANES)]
    grid_spec = pltpu.PrefetchScalarGridSpec(
        num_scalar_prefetch=1, grid=(nseq,), in_specs=in_specs,
        out_specs=seq3(t_new, ATT_W),
        scratch_shapes=[pltpu.VMEM((2, npages, 2 * KV_W, PAGE_SIZE), F32),
                        pltpu.VMEM((2, npages, 2 * KV_W, PAGE_SIZE), F32),
                        pltpu.SemaphoreType.DMA((2, 2)),
                        pltpu.VMEM((nlane // LANES * 2 * KV_W, LANES), F32),
                        pltpu.VMEM((npages, nrow, PAGE_SIZE), F32)])
    return pl.pallas_call(
        functools.partial(_attn_sample_kernel, jidx_np=tb["jidx_np"], n_top=N_TOP),
        grid_spec=grid_spec,
        out_shape=jax.ShapeDtypeStruct((nseq, t_new, ATT_W), F32),
        compiler_params=pltpu.CompilerParams(dimension_semantics=("arbitrary",),
                                             vmem_limit_bytes=SAMPLE_VMEM_LIMIT),
    )(page_table, q.reshape(nseq, t_new, ATT_W), gates.reshape(nseq, t_new, LANES),
      kvs.reshape(nseq, t_new, 2 * KV_W), kvw.reshape(nseq, t_new, 2 * KV_W), ckv0.reshape(nseq, 1, 2 * KV_W),
      _dims_major(cache_win), _dims_major(cache_cmp), _dims_major(cache_sel), cpt, wts["wc"].T, cc, cs1, cs2,
      tb["cmask"], tb["cmask_new"], tb["cand"], tb["forced"], tb["nmask"], tb["wmask"], tb["wmask_new"],
      tb["jidx"], tb["eeo"], tb["gsum"])


def _sample_group(x_sample, cache_cmp, cache_sel, cache_win, state_pool, page_table, g_pre_mix, g_post_mix,
                  g_pre_ffn, g_post_ffn, wts):
    nseq, t_new, _ = x_sample.shape
    past = page_table.shape[1] * PAGE_SIZE
    wb = cache_win.shape[1]
    x2 = x_sample.reshape(nseq * t_new, D_MODEL)
    hist16 = jnp.pad(state_pool, ((0, 0), (HIST_ROWS - POOL_HIST, 0), (0, 0)))
    q, kvc, kvs, kvw, gates, pool_y, znew, ckv0 = _inproj_sample(x2, g_pre_mix, wts, hist16, nseq, t_new, past)
    o_att = _attn_sample(q, gates, kvs, kvw, ckv0, cache_cmp, cache_sel, cache_win, page_table, wts, t_new)
    y = _post(x2, o_att.reshape(nseq * t_new, ATT_W), pool_y, g_post_mix, g_pre_ffn, g_post_ffn, wts)
    kv6 = lambda a: a.reshape(1, nseq, t_new, 2, KV_HEADS, HEAD_DIM)
    wkeep = min(WINDOW, wb + t_new)
    win_new = jnp.concatenate([cache_win[:, wb + t_new - wkeep:], kv6(kvw)[0]], axis=1)[None]
    return (y.reshape(nseq, t_new, D_MODEL), kv6(kvc), kv6(kvs), win_new, znew[None, :, 1:, :])


def kernel(x_prompt, x_sample, cache_kv_cmp, cache_kv_sel, cache_kv_win, state_pool, page_table, g_pre_mix,
           g_post_mix, g_pre_ffn, g_post_ffn, w_in, cmp_pos, w_cmp, w_pool, pool_scale, w_out, w_gate, w_up, w_down):
    wts = _prep_weights(w_in[0], cmp_pos[0], w_cmp[0], w_pool[0], pool_scale[0], w_out[0], w_gate[0], w_up[0],
                        w_down[0])
    yp, cmp_p, sel_p, win_p, pool_p = _prompt_group(x_prompt, g_pre_mix[0], g_post_mix[0], g_pre_ffn[0],
                                                    g_post_ffn[0], wts)
    ys, cmp_s, sel_s, win_s, pool_s = _sample_group(x_sample, cache_kv_cmp[0], cache_kv_sel[0], cache_kv_win[0],
                                                    state_pool[0], page_table, g_pre_mix[0], g_post_mix[0],
                                                    g_pre_ffn[0], g_post_ffn[0], wts)
    return (yp, ys, cmp_p, sel_p, win_p, pool_p, cmp_s, sel_s, win_s, pool_s)
```

```python
import functools

import jax
import jax.numpy as jnp
import numpy as np
from jax import lax
from jax.experimental import pallas as pl
from jax.experimental.pallas import tpu as pltpu

F32 = jnp.float32
MXU_DTYPE = jnp.bfloat16

D_MODEL = 1024
PAGE_SIZE = 128
N_HEADS = 8
KV_HEADS = 2
HPG = N_HEADS // KV_HEADS
HEAD_DIM = 64
ATT_W = N_HEADS * HEAD_DIM
KV_W = KV_HEADS * HEAD_DIM
N_BRANCH = 3
L_CMP = 32
L_SEL = 64
N_SEL = 16
N_FORCED = 3
N_TOP = N_SEL - N_FORCED
WINDOW = 512
SCALE = HEAD_DIM ** -0.5
ROT_DIM = HEAD_DIM // 4
ROT_HALF = ROT_DIM // 2
ROPE_THETA = 500000.0
POOL_WINDOWS = (2, 4, 8, 16)
POOL_W = D_MODEL - ATT_W
POOL_GC = POOL_W // len(POOL_WINDOWS)
POOL_HIST = max(POOL_WINDOWS) - 1
HIST_ROWS = POOL_HIST + 1
N_GATE = N_HEADS * N_BRANCH
D_FF = -(-8 * D_MODEL // (3 * 256)) * 256
EPS = 1e-6
NEG = -1e30

LANES = 128
C_Q = 0
C_KVC = C_Q + ATT_W
C_KVS = C_KVC + 2 * KV_W
C_KVW = C_KVS + 2 * KV_W
C_POOL = C_KVW + 2 * KV_W
C_GATE = C_POOL + POOL_W
IN_W_PAD = C_GATE + LANES

TM_IN = 512
TM_POST = 1024
Q_BLK = 256
K_CHUNK = 512
FF_CHUNKS = 11
VMEM_LIMIT = 56 * 1024 * 1024
SAMPLE_VMEM_LIMIT = 60 * 1024 * 1024


def _const_spec(shape):
    nd = len(shape)
    return pl.BlockSpec(shape, lambda *_: (0,) * nd, pipeline_mode=pl.Buffered(1))


def _rms(x, g):
    return x * lax.rsqrt(jnp.mean(x * x, axis=-1, keepdims=True) + EPS) * g


def _rope128(x, c, s1, s2):
    return x * c + pltpu.roll(x, ROT_HALF, 1) * s1 + pltpu.roll(x, LANES - ROT_HALF, 1) * s2


def _dot(a, b):
    return jnp.dot(a, b, preferred_element_type=F32)


def _dot_nt(a, b):
    return lax.dot_general(a, b, (((1,), (1,)), ((), ())), preferred_element_type=F32)


def _dot_tn(a, b):
    return lax.dot_general(a, b, (((0,), (0,)), ((), ())), preferred_element_type=F32)


def _rope_tables(pos):
    pos = jnp.asarray(np.asarray(pos), jnp.int32)
    inv = ROPE_THETA ** (-jnp.arange(ROT_HALF, dtype=F32) * 2.0 / ROT_DIM)
    ang = pos.astype(F32)[:, None] * inv
    cos, sin = jnp.cos(ang), jnp.sin(ang)
    n = pos.shape[0]
    one = jnp.ones((n, HEAD_DIM - ROT_DIM), F32)
    zero = jnp.zeros((n, HEAD_DIM - ROT_DIM), F32)
    zh = jnp.zeros((n, ROT_HALF), F32)
    c = jnp.concatenate([cos, cos, one], axis=1)
    s1 = jnp.concatenate([zh, sin, zero], axis=1)
    s2 = jnp.concatenate([-sin, zh, zero], axis=1)
    tile = lambda t: jnp.concatenate([t, t], axis=1)
    return tile(c), tile(s1), tile(s2)


def _prep_weights(w_in, cmp_pos, w_cmp, w_pool, pool_scale, w_out, w_gate, w_up, w_down):
    w_in_r = jnp.concatenate([
        w_in[:, :ATT_W + 6 * KV_W],
        w_in[:, ATT_W + 6 * KV_W + N_GATE:],
        w_in[:, ATT_W + 6 * KV_W:ATT_W + 6 * KV_W + N_GATE],
        jnp.zeros((D_MODEL, LANES - N_GATE), w_in.dtype)], axis=1).astype(MXU_DTYPE)
    cp = jnp.concatenate([cmp_pos[:, 0], cmp_pos[:, 0], cmp_pos[:, 1], cmp_pos[:, 1]], axis=1)
    wc = jnp.zeros((2 * KV_W, 2 * KV_W), F32)
    for i, c in enumerate((0, 0, 1, 1)):
        wc = wc.at[i * HEAD_DIM:(i + 1) * HEAD_DIM, i * HEAD_DIM:(i + 1) * HEAD_DIM].set(w_cmp[c])
    fc = D_FF // FF_CHUNKS
    return dict(
        w_in=w_in_r, cp=cp, wc=wc.astype(MXU_DTYPE), w_pool=w_pool.astype(MXU_DTYPE),
        pool_scale=pool_scale.reshape(1, POOL_W), w_out=w_out.astype(MXU_DTYPE),
        w_gate=w_gate.reshape(D_MODEL, FF_CHUNKS, fc).transpose(1, 0, 2).astype(MXU_DTYPE),
        w_up=w_up.reshape(D_MODEL, FF_CHUNKS, fc).transpose(1, 0, 2).astype(MXU_DTYPE),
        w_down=w_down.reshape(FF_CHUNKS, fc, D_MODEL).astype(MXU_DTYPE))


def _project_rows(x, g, w_ref, rc, rs1, rs2):
    h = _rms(x, g).astype(MXU_DTYPE)
    z = _dot(h, w_ref[...])
    q = jnp.concatenate(
        [_rope128(z[:, C_Q + i * LANES:C_Q + (i + 1) * LANES], rc, rs1, rs2) for i in range(ATT_W // LANES)],
        axis=1) * SCALE
    kvc = z[:, C_KVC:C_KVS]
    kvs = jnp.concatenate([_rope128(z[:, C_KVS:C_KVS + KV_W], rc, rs1, rs2), z[:, C_KVS + KV_W:C_KVW]], axis=1)
    kvw = jnp.concatenate([_rope128(z[:, C_KVW:C_KVW + KV_W], rc, rs1, rs2), z[:, C_KVW + KV_W:C_POOL]], axis=1)
    u = z[:, C_POOL:C_GATE]
    gates = z[:, C_GATE:IN_W_PAD]
    return q, kvc, kvs, kvw, u, gates


def _compress_rows(kvc, cp, wc_ref, cc, cs1, cs2):
    n = kvc.shape[0] // L_CMP
    summ = jnp.sum(kvc.reshape(n, L_CMP, 2 * KV_W) * cp[None], axis=1)
    ckv = _dot(summ.astype(MXU_DTYPE), wc_ref[...])
    return jnp.concatenate([_rope128(ckv[:, :KV_W], cc, cs1, cs2), ckv[:, KV_W:]], axis=1)


def _pool_windows(zf, rows, cnt):
    outs = []
    for gi, w in enumerate(POOL_WINDOWS):
        a = zf[:, gi * POOL_GC:(gi + 1) * POOL_GC]
        s, sh = a, 1
        while sh < w:
            s = s + pltpu.roll(s, sh, 0)
            sh *= 2
        outs.append(rows(s) / cnt[gi] - rows(a))
    return outs


def _pool_project(d_list, wp_ref, ps):
    y = jnp.concatenate([_dot(d.astype(MXU_DTYPE), wp_ref[gi]) for gi, d in enumerate(d_list)], axis=1)
    return y * ps


def _inproj_prompt_kernel(x_ref, g_ref, w_ref, rc_ref, rs1_ref, rs2_ref, cp_ref, wc_ref, cc_ref, cs1_ref, cs2_ref,
                          wp_ref, ps_ref,
                          q_ref, kvc_ref, kvs_ref, kvw_ref, kvsb_ref, kvwb_ref, gates_ref, pool_ref, ulast_ref,
                          ckv_ref, z_scr):
    j = pl.program_id(1)
    tm = x_ref.shape[0]
    nblk = tm // L_CMP
    r0 = pl.multiple_of(j * tm, tm)
    rows = pl.ds(r0, tm)
    q, kvc, kvs, kvw, u, gates = _project_rows(x_ref[...], g_ref[...], w_ref, rc_ref[rows, :], rs1_ref[rows, :],
                                               rs2_ref[rows, :])
    q_ref[...] = q.astype(q_ref.dtype)
    kvc_ref[0] = kvc.T
    kvs_ref[0] = kvs.T
    kvw_ref[0] = kvw.T
    kvsb_ref[...] = kvs.astype(kvsb_ref.dtype)
    kvwb_ref[...] = kvw.astype(kvwb_ref.dtype)
    gates_ref[...] = gates

    crow = pl.ds(pl.multiple_of(j * nblk, nblk), nblk)
    ckv_ref[...] = _compress_rows(kvc, cp_ref[...], wc_ref, cc_ref[crow, :], cs1_ref[crow, :], cs2_ref[crow, :])

    @pl.when(j == 0)
    def _():
        z_scr[0:HIST_ROWS, :] = jnp.zeros((HIST_ROWS, POOL_W), F32)

    @pl.when(j > 0)
    def _():
        z_scr[0:HIST_ROWS, :] = z_scr[tm:tm + HIST_ROWS, :]

    z_scr[HIST_ROWS:, :] = u
    pos = (r0 + lax.broadcasted_iota(jnp.int32, (tm, 1), 0)).astype(F32)
    cnt = [jnp.minimum(pos + 1.0, float(w)) for w in POOL_WINDOWS]
    d_list = _pool_windows(z_scr[...], lambda a: a[HIST_ROWS:, :], cnt)
    pool_ref[...] = _pool_project(d_list, wp_ref, ps_ref[...]).astype(pool_ref.dtype)
    ulast_ref[0] = u[tm - HIST_ROWS:, :]


def _inproj_prompt(x2, g_pre, wts, seq, batch):
    n = x2.shape[0]
    tm = min(TM_IN, seq)
    nj = seq // tm
    nblk = tm // L_CMP
    rc, rs1, rs2 = _rope_tables(np.arange(seq))
    cc, cs1, cs2 = _rope_tables((np.arange(seq // L_CMP) + 1) * L_CMP - 1)
    row = lambda w: pl.BlockSpec((tm, w), lambda b, j: (b * nj + j, 0))
    out_shape = (
        jax.ShapeDtypeStruct((n, ATT_W), MXU_DTYPE),
        jax.ShapeDtypeStruct((batch, 2 * KV_W, seq), F32),
        jax.ShapeDtypeStruct((batch, 2 * KV_W, seq), F32),
        jax.ShapeDtypeStruct((batch, 2 * KV_W, seq), F32),
        jax.ShapeDtypeStruct((n, 2 * KV_W), MXU_DTYPE),
        jax.ShapeDtypeStruct((n, 2 * KV_W), MXU_DTYPE),
        jax.ShapeDtypeStruct((n, LANES), F32),
        jax.ShapeDtypeStruct((n, POOL_W), MXU_DTYPE),
        jax.ShapeDtypeStruct((batch, HIST_ROWS, POOL_W), F32),
        jax.ShapeDtypeStruct((n // L_CMP, 2 * KV_W), F32),
    )
    dims_major = pl.BlockSpec((1, 2 * KV_W, tm), lambda b, j: (b, 0, j))
    out_specs = (row(ATT_W), dims_major, dims_major, dims_major, row(2 * KV_W), row(2 * KV_W), row(LANES),
                 row(POOL_W), pl.BlockSpec((1, HIST_ROWS, POOL_W), lambda b, j: (b, 0, 0)),
                 pl.BlockSpec((nblk, 2 * KV_W), lambda b, j: (b * nj + j, 0)))
    in_specs = [row(D_MODEL), _const_spec((1, D_MODEL)), _const_spec((D_MODEL, IN_W_PAD)),
                _const_spec((seq, LANES)), _const_spec((seq, LANES)), _const_spec((seq, LANES)),
                _const_spec((L_CMP, 2 * KV_W)), _const_spec((2 * KV_W, 2 * KV_W)),
                _const_spec((seq // L_CMP, LANES)), _const_spec((seq // L_CMP, LANES)),
                _const_spec((seq // L_CMP, LANES)),
                _const_spec((len(POOL_WINDOWS), POOL_GC, POOL_GC)), _const_spec((1, POOL_W))]
    return pl.pallas_call(
        _inproj_prompt_kernel,
        grid=(batch, nj),
        in_specs=in_specs, out_specs=out_specs, out_shape=out_shape,
        scratch_shapes=[pltpu.VMEM((tm + HIST_ROWS, POOL_W), F32)],
        compiler_params=pltpu.CompilerParams(dimension_semantics=("arbitrary", "arbitrary"),
                                             vmem_limit_bytes=VMEM_LIMIT),
    )(x2, g_pre.reshape(1, D_MODEL), wts["w_in"], rc, rs1, rs2, wts["cp"], wts["wc"], cc, cs1, cs2,
      wts["w_pool"], wts["pool_scale"])


def _post_kernel(x_ref, o_ref, p_ref, wo_ref, gpm_ref, gpf_ref, gqf_ref, wg_ref, wu_ref, wd_ref, y_ref):
    mixed = _dot(jnp.concatenate([o_ref[...].astype(MXU_DTYPE), p_ref[...]], axis=1), wo_ref[...])
    x1 = x_ref[...] + _rms(mixed, gpm_ref[...])
    h2 = _rms(x1, gpf_ref[...]).astype(MXU_DTYPE)
    acc = None
    for c in range(FF_CHUNKS):
        f = (jax.nn.silu(_dot(h2, wg_ref[c])) * _dot(h2, wu_ref[c])).astype(MXU_DTYPE)
        part = _dot(f, wd_ref[c])
        acc = part if acc is None else acc + part
    y_ref[...] = x1 + _rms(acc, gqf_ref[...])


def _post(x2, o_att, pool_y, g_post_mix, g_pre_ffn, g_post_ffn, wts):
    n = x2.shape[0]
    tm = min(TM_POST, n)
    fc = D_FF // FF_CHUNKS
    row = lambda w: pl.BlockSpec((tm, w), lambda i: (i, 0))
    return pl.pallas_call(
        _post_kernel,
        grid=(n // tm,),
        in_specs=[row(D_MODEL), row(ATT_W), row(POOL_W), _const_spec((D_MODEL, D_MODEL)),
                  _const_spec((1, D_MODEL)), _const_spec((1, D_MODEL)), _const_spec((1, D_MODEL)),
                  _const_spec((FF_CHUNKS, D_MODEL, fc)), _const_spec((FF_CHUNKS, D_MODEL, fc)),
                  _const_spec((FF_CHUNKS, fc, D_MODEL))],
        out_specs=row(D_MODEL),
        out_shape=jax.ShapeDtypeStruct((n, D_MODEL), F32),
        compiler_params=pltpu.CompilerParams(dimension_semantics=("arbitrary",), vmem_limit_bytes=VMEM_LIMIT),
    )(x2, o_att, pool_y, wts["w_out"], g_post_mix.reshape(1, D_MODEL), g_pre_ffn.reshape(1, D_MODEL),
      g_post_ffn.reshape(1, D_MODEL), wts["w_gate"], wts["w_up"], wts["w_down"])


def _unselected_blocks_t(p_slc_t, c_row, n_top, need_rank):
    nb = p_slc_t.shape[0]
    jj = lax.broadcasted_iota(jnp.int32, p_slc_t.shape, 0)

    def ranked():
        cand = (jj >= 1) & (jj <= c_row - 2)
        sc = jnp.where(cand, p_slc_t, -1.0)
        rank = jnp.zeros(sc.shape, F32)
        for i in range(nb):
            row = sc[i:i + 1, :]
            beats = (row > sc) | ((row == sc) & (jj > i))
            rank = rank + jnp.where(beats, 1.0, 0.0)
        sel = (cand & (rank < n_top)) | (jj == 0) | (jj == c_row) | (jj == c_row - 1)
        return jnp.where(sel, 0.0, 1.0)

    def all_candidates():
        return jnp.where(jj <= c_row, 0.0, 1.0)

    return lax.cond(need_rank, ranked, all_candidates)


def _lane_tiles(a, n):
    return jnp.concatenate([a] * n, axis=1)


def _attn_prompt_kernel(q_ref, gt_ref, ckv_ref, ks_ref, kw_ref, eneg_ref, o_ref,
                        qaug_scr, oc_scr, s_scr, mel_scr, mb_scr, lel_scr, acc_scr):
    qb = pl.program_id(1)
    qblk = q_ref.shape[0]
    rows = HPG * qblk
    ncb = ckv_ref.shape[1]
    half = ncb // 2
    wspan = WINDOW + qblk
    start = qb * qblk
    tpos_c = start + lax.broadcasted_iota(jnp.int32, (qblk, 1), 0)
    tpos_r = start + lax.broadcasted_iota(jnp.int32, (1, qblk), 1)
    gs = jax.nn.sigmoid(gt_ref[...])
    ckv = ckv_ref[0]
    ck = ckv[:, :KV_W].astype(MXU_DTYPE)
    cv = ckv[:, KV_W:].astype(MXU_DTYPE)
    crow = lax.broadcasted_iota(jnp.int32, (ncb, rows), 0)
    cblk = 2 * (crow % half) + crow // half
    cmask = ((cblk + 1) * L_CMP - 1) <= _lane_tiles(tpos_r, HPG)

    ws = pl.multiple_of(jnp.maximum(start - WINDOW, 0), qblk)
    dwin = tpos_c - (ws + lax.broadcasted_iota(jnp.int32, (1, wspan), 1))
    bias_w = jnp.where((dwin >= 0) & (dwin < WINDOW), 0.0, NEG)
    n_chunks = start // K_CHUNK + 1
    lane = lax.broadcasted_iota(jnp.int32, (qblk, LANES), 1)

    outs = [None] * N_HEADS
    for g in range(KV_HEADS):
        pieces = []
        for h in range(HPG):
            hh = g * HPG + h
            tile = q_ref[:, (hh // 2) * LANES:(hh // 2 + 1) * LANES].astype(F32)
            if hh % 2 != g:
                tile = pltpu.roll(tile, HEAD_DIM, 1)
            pieces.append(jnp.where((lane >= g * HEAD_DIM) & (lane < (g + 1) * HEAD_DIM), tile, 0.0))
        qg = jnp.concatenate(pieces, axis=0).astype(MXU_DTYPE)

        s = jnp.where(cmask, _dot_nt(ck, qg), NEG)
        e = jnp.where(cmask, jnp.exp(s - jnp.max(s, axis=0, keepdims=True)), 0.0)
        p = e / jnp.maximum(jnp.sum(e, axis=0, keepdims=True), 1e-30)
        o_c = _dot_tn(p.astype(MXU_DTYPE), cv)
        pc = p[:, 0:qblk]
        for h in range(1, HPG):
            pc = pc + p[:, h * qblk:(h + 1) * qblk]
        need_rank = (start + qblk - 1) // L_SEL - 2 > N_TOP
        unsel = _unselected_blocks_t(pc[0:half] + pc[half:ncb], tpos_r // L_SEL, N_TOP, need_rank)
        unsel = jnp.concatenate([unsel, jnp.zeros((LANES - half, qblk), F32)], axis=0)
        unsel = unsel.T.astype(MXU_DTYPE)
        gr = slice(g * rows, (g + 1) * rows)
        qaug_scr[gr, :] = jnp.concatenate([qg, jnp.concatenate([unsel] * HPG, axis=0)], axis=1)
        oc_scr[gr, :] = o_c

    mel_scr[...] = jnp.full((KV_HEADS * rows, LANES), NEG, F32)

    def score_chunk(ci, causal):
        k0 = pl.multiple_of(ci * K_CHUNK, K_CHUNK)
        kaug = jnp.concatenate([ks_ref[0, pl.ds(k0, K_CHUNK), 0:KV_W], eneg_ref[pl.ds(k0, K_CHUNK), :]], axis=1)
        sv = _dot_nt(qaug_scr[...], kaug)
        if causal:
            kpos = k0 + lax.broadcasted_iota(jnp.int32, (1, K_CHUNK), 1)
            bias = jnp.where(kpos <= tpos_c, 0.0, NEG)
            sv = jnp.concatenate([sv[r * qblk:(r + 1) * qblk] + bias for r in range(KV_HEADS * HPG)], axis=0)
        s_scr[ci] = sv
        mh = sv[:, 0:LANES]
        for j in range(1, K_CHUNK // LANES):
            mh = jnp.maximum(mh, sv[:, j * LANES:(j + 1) * LANES])
        mel_scr[...] = jnp.maximum(mel_scr[...], mh)

    def score_body(ci, _):
        score_chunk(ci, False)
        return 0

    lax.fori_loop(0, n_chunks - 1, score_body, 0)
    score_chunk(n_chunks - 1, True)
    mb_scr[...] = jnp.broadcast_to(jnp.max(mel_scr[...], axis=1, keepdims=True), (KV_HEADS * rows, LANES))
    lel_scr[...] = jnp.zeros((KV_HEADS * rows, LANES), F32)
    acc_scr[...] = jnp.zeros((KV_HEADS * rows, LANES), F32)

    def value_body(ci, _):
        k0 = pl.multiple_of(ci * K_CHUNK, K_CHUNK)
        pv = jnp.exp(s_scr[ci] - _lane_tiles(mb_scr[...], K_CHUNK // LANES))
        lsum = pv[:, 0:LANES]
        for j in range(1, K_CHUNK // LANES):
            lsum = lsum + pv[:, j * LANES:(j + 1) * LANES]
        lel_scr[...] += lsum
        acc_scr[...] += _dot(pv.astype(MXU_DTYPE), ks_ref[0, pl.ds(k0, K_CHUNK), KV_W:2 * KV_W])
        return 0

    lax.fori_loop(0, n_chunks, value_body, 0)
    acc_scr[...] = acc_scr[...] / jnp.maximum(jnp.sum(lel_scr[...], axis=1, keepdims=True), 1e-30)

    for g in range(KV_HEADS):
        gr = slice(g * rows, (g + 1) * rows)
        o_c = oc_scr[gr, :]
        o_s = acc_scr[gr, :]
        qg = qaug_scr[gr, 0:KV_W]
        sw = _dot_nt(qg, kw_ref[0, pl.ds(ws, wspan), 0:KV_W])
        sw = jnp.concatenate([sw[h * qblk:(h + 1) * qblk] + bias_w for h in range(HPG)], axis=0)
        ew = jnp.exp(sw - jnp.max(sw, axis=1, keepdims=True))
        o_w = (_dot(ew.astype(MXU_DTYPE), kw_ref[0, pl.ds(ws, wspan), KV_W:2 * KV_W])
               / jnp.maximum(jnp.sum(ew, axis=1, keepdims=True), 1e-30))

        for h in range(HPG):
            hh = g * HPG + h
            col = hh * N_BRANCH
            rs = slice(h * qblk, (h + 1) * qblk)
            cs = slice(g * HEAD_DIM, (g + 1) * HEAD_DIM)
            outs[hh] = (gs[:, col:col + 1] * o_c[rs, cs] + gs[:, col + 1:col + 2] * o_s[rs, cs]
                        + gs[:, col + 2:col + 3] * o_w[rs, cs])
    o_ref[...] = jnp.concatenate(outs, axis=1).astype(o_ref.dtype)


def _attn_prompt(q, gates, ckv_perm, kvsb, kvwb, seq, batch):
    n = q.shape[0]
    qblk = min(Q_BLK, seq)
    nqb = seq // qblk
    nb = seq // L_SEL
    nchunk = seq // K_CHUNK
    rows = HPG * qblk
    assert seq >= WINDOW + qblk and seq % K_CHUNK == 0 and K_CHUNK % qblk == 0
    assert nb <= LANES
    eneg = jnp.asarray(NEG * (np.arange(seq)[:, None] // L_SEL == np.arange(LANES)[None, :]), MXU_DTYPE)
    row = lambda w: pl.BlockSpec((qblk, w), lambda b, i: (b * nqb + i, 0))
    per_b = lambda r, w: pl.BlockSpec((1, r, w), lambda b, i: (b, 0, 0))
    return pl.pallas_call(
        _attn_prompt_kernel,
        grid=(batch, nqb),
        in_specs=[row(ATT_W), row(LANES), per_b(seq // L_CMP, 2 * KV_W), per_b(seq, 2 * KV_W),
                  per_b(seq, 2 * KV_W), _const_spec((seq, LANES))],
        out_specs=row(ATT_W),
        out_shape=jax.ShapeDtypeStruct((n, ATT_W), MXU_DTYPE),
        scratch_shapes=[pltpu.VMEM((KV_HEADS * rows, 2 * LANES), MXU_DTYPE),
                        pltpu.VMEM((KV_HEADS * rows, LANES), F32),
                        pltpu.VMEM((nchunk, KV_HEADS * rows, K_CHUNK), F32)]
        + [pltpu.VMEM((KV_HEADS * rows, LANES), F32)] * 4,
        compiler_params=pltpu.CompilerParams(dimension_semantics=("arbitrary", "arbitrary"),
                                             vmem_limit_bytes=VMEM_LIMIT),
    )(q, gates, ckv_perm, kvsb.reshape(batch, seq, 2 * KV_W), kvwb.reshape(batch, seq, 2 * KV_W), eneg)


def _prompt_group(x_prompt, g_pre_mix, g_post_mix, g_pre_ffn, g_post_ffn, wts):
    batch, seq, _ = x_prompt.shape
    x2 = x_prompt.reshape(batch * seq, D_MODEL)
    q, kvc, kvs, kvw, kvsb, kvwb, gates, pool_y, ulast, ckv = _inproj_prompt(x2, g_pre_mix, wts, seq, batch)
    ncb = seq // L_CMP
    ckv_perm = ckv.reshape(batch, ncb // 2, 2, 2 * KV_W).transpose(0, 2, 1, 3).reshape(batch, ncb, 2 * KV_W)
    o_att = _attn_prompt(q, gates, ckv_perm, kvsb, kvwb, seq, batch)
    y = _post(x2, o_att, pool_y, g_post_mix, g_pre_ffn, g_post_ffn, wts)
    wkeep = min(WINDOW, seq)
    kv6 = lambda a: jnp.transpose(a.reshape(1, batch, 2, KV_HEADS, HEAD_DIM, a.shape[-1]), (0, 1, 5, 2, 3, 4))
    return (y.reshape(batch, seq, D_MODEL), kv6(kvc), kv6(kvs), kv6(kvw[:, :, seq - wkeep:]),
            ulast[None, :, 1:, :])


def _inproj_sample_kernel(x_ref, g_ref, w_ref, rc_ref, rs1_ref, rs2_ref, cp_ref, wc_ref, cc_ref, cs1_ref, cs2_ref,
                          wp_ref, ps_ref, hist_ref,
                          q_ref, kvc_ref, kvs_ref, kvw_ref, gates_ref, pool_ref, znew_ref, ckv_ref, z_scr, *, past):
    nseq, zrows, _ = z_scr.shape
    t_new = zrows - HIST_ROWS
    n = nseq * t_new
    q, kvc, kvs, kvw, u, gates = _project_rows(x_ref[...], g_ref[...], w_ref, rc_ref[...], rs1_ref[...],
                                               rs2_ref[...])
    q_ref[...] = q
    kvc_ref[...] = kvc
    kvs_ref[...] = kvs
    kvw_ref[...] = kvw
    gates_ref[...] = gates

    summ = jnp.sum(kvc.reshape(nseq, t_new, 2 * KV_W) * cp_ref[0:t_new, :][None], axis=1)
    ckv = _dot(summ.astype(MXU_DTYPE), wc_ref[...])
    ckv_ref[...] = jnp.concatenate([_rope128(ckv[:, :KV_W], cc_ref[...], cs1_ref[...], cs2_ref[...]),
                                    ckv[:, KV_W:]], axis=1)

    z_scr[:, 0:HIST_ROWS, :] = hist_ref[...]
    z_scr[:, HIST_ROWS:, :] = u.reshape(nseq, t_new, POOL_W)
    zf = z_scr[...].reshape(nseq * zrows, POOL_W)
    tok = lax.broadcasted_iota(jnp.int32, (n, 1), 0) % t_new
    pos = (past + tok).astype(F32)
    cnt = [jnp.minimum(pos + 1.0, float(w)) for w in POOL_WINDOWS]
    take = lambda a: a.reshape(nseq, zrows, POOL_GC)[:, HIST_ROWS:, :].reshape(n, POOL_GC)
    d_list = _pool_windows(zf, take, cnt)
    pool_ref[...] = _pool_project(d_list, wp_ref, ps_ref[...]).astype(pool_ref.dtype)
    znew_ref[...] = z_scr[:, zrows - HIST_ROWS:, :]


def _inproj_sample(x2, g_pre, wts, hist16, nseq, t_new, past):
    n = x2.shape[0]
    rc, rs1, rs2 = _rope_tables(np.tile(past + np.arange(t_new), nseq))
    cc, cs1, cs2 = _rope_tables(np.array([past + L_CMP - 1]))
    full = lambda *s: pl.BlockSpec(s, lambda i: (0,) * len(s))
    out_shape = (
        jax.ShapeDtypeStruct((n, ATT_W), F32),
        jax.ShapeDtypeStruct((n, 2 * KV_W), F32),
        jax.ShapeDtypeStruct((n, 2 * KV_W), F32),
        jax.ShapeDtypeStruct((n, 2 * KV_W), F32),
        jax.ShapeDtypeStruct((n, LANES), F32),
        jax.ShapeDtypeStruct((n, POOL_W), MXU_DTYPE),
        jax.ShapeDtypeStruct((nseq, HIST_ROWS, POOL_W), F32),
        jax.ShapeDtypeStruct((nseq, 2 * KV_W), F32),
    )
    out_specs = (full(n, ATT_W), full(n, 2 * KV_W), full(n, 2 * KV_W), full(n, 2 * KV_W), full(n, LANES),
                 full(n, POOL_W), full(nseq, HIST_ROWS, POOL_W), full(nseq, 2 * KV_W))
    in_specs = [full(n, D_MODEL), full(1, D_MODEL), full(D_MODEL, IN_W_PAD), full(n, LANES), full(n, LANES),
                full(n, LANES), full(L_CMP, 2 * KV_W), full(2 * KV_W, 2 * KV_W), full(1, LANES), full(1, LANES),
                full(1, LANES), full(len(POOL_WINDOWS), POOL_GC, POOL_GC), full(1, POOL_W),
                full(nseq, HIST_ROWS, POOL_W)]
    return pl.pallas_call(
        functools.partial(_inproj_sample_kernel, past=past),
        grid=(1,),
        in_specs=in_specs, out_specs=out_specs, out_shape=out_shape,
        scratch_shapes=[pltpu.VMEM((nseq, HIST_ROWS + t_new, POOL_W), F32)],
        compiler_params=pltpu.CompilerParams(dimension_semantics=("arbitrary",), vmem_limit_bytes=VMEM_LIMIT),
    )(x2, g_pre.reshape(1, D_MODEL), wts["w_in"], rc, rs1, rs2, wts["cp"], wts["wc"], cc, cs1, cs2,
      wts["w_pool"], wts["pool_scale"], hist16)


PAGE_CHUNK = 8
CMP_LANE_BLKS = PAGE_SIZE // L_CMP
HALF_PAGES = LANES // CMP_LANE_BLKS
CMP_PAGES_PER_STEP = 4


def _sample_tables(past, t_new, wb):
    npages = past // PAGE_SIZE
    nrow = HPG * KV_HEADS * t_new
    r = np.arange(nrow)
    t_row = r % t_new
    pos = past + t_row
    lanes = np.arange(npages * CMP_LANE_BLKS)
    page = HALF_PAGES * (lanes // LANES) + lanes % HALF_PAGES
    m = (lanes % LANES) // HALF_PAGES
    blk = CMP_LANE_BLKS * page + (m - 1) % CMP_LANE_BLKS
    cmask = (blk[None, :] + 1) * L_CMP - 1 <= pos[:, None]
    n_new = 2 * t_new
    blk_new = past // L_CMP + np.arange(n_new)
    n_cmp = (past + -(-t_new // L_SEL) * L_SEL) // L_CMP
    cmask_new = ((blk_new[None, :] + 1) * L_CMP - 1 <= pos[:, None]) & (blk_new[None, :] < n_cmp)
    jidx = np.concatenate([2 * np.arange(npages), 2 * np.arange(npages) + 1])
    rg = np.arange(KV_HEADS * t_new)
    c = ((past + rg % t_new) // L_SEL)[:, None]
    jj = jidx[None, :]
    cand = (jj >= 1) & (jj <= c - 2)
    forced = (jj == 0) | (jj == c) | (jj == c - 1)
    l = np.arange(n_new)
    new_blk = past // L_SEL
    c_row = (pos // L_SEL)[:, None]
    new_forced = (new_blk == 0) | (new_blk == c_row) | (new_blk == c_row - 1)
    nmask = new_forced & (l[None, :] <= t_row[:, None]) & (l[None, :] < t_new)
    i = np.arange(wb)
    d = t_row[:, None] + wb - i[None, :]
    wmask = (d >= 0) & (d < WINDOW) & (past - wb + i[None, :] >= 0)
    dn = t_row[:, None] - l[None, :]
    wmask_new = (dn >= 0) & (dn < WINDOW) & (l[None, :] < t_new)
    ck = PAGE_CHUNK * PAGE_SIZE
    k = np.arange(ck)
    eeo = np.zeros((npages // PAGE_CHUNK, 2 * npages, ck), np.float32)
    for ch in range(npages // PAGE_CHUNK):
        pg = PAGE_CHUNK * ch + k // PAGE_SIZE
        upper = (k % PAGE_SIZE) >= L_SEL
        eeo[ch, pg[~upper], k[~upper]] = 1.0
        eeo[ch, npages + pg[upper], k[upper]] = 1.0
    tt = np.arange(PAGE_SIZE)
    steps = HALF_PAGES // CMP_PAGES_PER_STEP
    gsum = np.zeros((steps, CMP_PAGES_PER_STEP * PAGE_SIZE, LANES), np.float32)
    for it in range(steps):
        for kk in range(CMP_PAGES_PER_STEP):
            col = HALF_PAGES * ((tt // L_CMP + 1) % CMP_LANE_BLKS) + it * CMP_PAGES_PER_STEP + kk
            gsum[it, kk * PAGE_SIZE + tt, col] = 1.0
    f = lambda a: jnp.asarray(np.asarray(a, np.float32))
    return dict(gsum=jnp.asarray(gsum, MXU_DTYPE), cmask=f(cmask), cmask_new=f(cmask_new), cand=f(cand), forced=f(forced), nmask=f(nmask),
                wmask=f(wmask), wmask_new=f(wmask_new), jidx=jnp.asarray(jidx[None, :], jnp.int32),
                eeo=jnp.asarray(eeo, MXU_DTYPE), blk=blk, jidx_np=jidx)


def _rope_tables_t(pos):
    c, s1, s2 = _rope_tables(pos)
    return c.T, s1.T, s2.T


def _attn_sample_kernel(pt_ref, q_ref, gt_ref, kvs_ref, kvw_ref, ckv0_ref, cw_ref, cmp_hbm, sel_hbm,
                        cpt_ref, wct_ref, cc_ref, cs1_ref, cs2_ref, cmask_ref, cmaskn_ref, cand_ref, forced_ref,
                        nmask_ref, wmask_ref, wmaskn_ref, jidx_ref, eeo_ref, gsum_ref,
                        o_ref, wout_ref,
                        cbuf, sbuf, sem, summ_scr, s_scr, *, jidx_np, n_top):
    b = pl.program_id(0)
    nseq = pl.num_programs(0)
    npages = pt_ref.shape[1]
    t_new = q_ref.shape[1]
    nrow = HPG * KV_HEADS * t_new
    ngt = KV_HEADS * t_new
    slot = b % 2

    def copies(seq, sl):
        out = []
        for p in range(npages):
            pg = pt_ref[seq, p]
            out.append(pltpu.make_async_copy(cmp_hbm.at[pg], cbuf.at[sl, p], sem.at[0, sl]))
            out.append(pltpu.make_async_copy(sel_hbm.at[pg], sbuf.at[sl, p], sem.at[1, sl]))
        return out

    @pl.when(b == 0)
    def _():
        for cpy in copies(b, slot):
            cpy.start()

    @pl.when(b + 1 < nseq)
    def _():
        for cpy in copies(b + 1, 1 - slot):
            cpy.start()

    qb = q_ref[0]
    zeros = jnp.zeros((t_new, HEAD_DIM), F32)
    blocks = []
    for h in range(HPG):
        for g in range(KV_HEADS):
            piece = qb[:, (g * HPG + h) * HEAD_DIM:(g * HPG + h + 1) * HEAD_DIM]
            blocks.append(jnp.concatenate([piece, zeros] if g == 0 else [zeros, piece], axis=1))
    qbd = jnp.concatenate(blocks, axis=0).astype(MXU_DTYPE)

    def new_rows(rows):
        a = jnp.concatenate([rows, jnp.zeros((2 * t_new - rows.shape[0], 2 * KV_W), F32)], axis=0)
        return a[:, :KV_W].astype(MXU_DTYPE), a[:, KV_W:].astype(MXU_DTYPE)

    def joint_softmax(s1, ok1, s2, ok2):
        s1 = jnp.where(ok1, s1, NEG)
        s2 = jnp.where(ok2, s2, NEG)
        m = jnp.maximum(jnp.max(s1, axis=1, keepdims=True), jnp.max(s2, axis=1, keepdims=True))
        e1 = jnp.where(ok1, jnp.exp(s1 - m), 0.0)
        e2 = jnp.where(ok2, jnp.exp(s2 - m), 0.0)
        l = jnp.maximum(jnp.sum(e1, axis=1, keepdims=True) + jnp.sum(e2, axis=1, keepdims=True), 1e-30)
        return e1 / l, e2 / l

    kwn, vwn = new_rows(kvw_ref[0])
    p1, p2 = joint_softmax(_dot(qbd, cw_ref[0, 0:KV_W, :].astype(MXU_DTYPE)), wmask_ref[...] > 0.5,
                           _dot_nt(qbd, kwn), wmaskn_ref[...] > 0.5)
    o_w = (_dot_nt(p1.astype(MXU_DTYPE), cw_ref[0, KV_W:2 * KV_W, :].astype(MXU_DTYPE))
           + _dot(p2.astype(MXU_DTYPE), vwn))

    wb = cw_ref.shape[2]
    shifted = pltpu.roll(cw_ref[0], wb - t_new, 1)
    new_t = jnp.concatenate([jnp.zeros((LANES - t_new, 2 * KV_W), F32), kvw_ref[0]], axis=0).T
    wlane = lax.broadcasted_iota(jnp.int32, (2 * KV_W, LANES), 1)
    wout_ref[0, :, 0:wb - LANES] = shifted[:, 0:wb - LANES]
    wout_ref[0, :, wb - LANES:wb] = jnp.where(wlane >= LANES - t_new, new_t, shifted[:, wb - LANES:wb])

    for cpy in copies(b, slot):
        cpy.wait()

    nhalf = npages // HALF_PAGES
    summ_scr[...] = jnp.zeros(summ_scr.shape, F32)

    def cmp_body(it, _):
        xs = []
        for k in range(CMP_PAGES_PER_STEP):
            pg = it * CMP_PAGES_PER_STEP + k
            x = jnp.concatenate([cbuf[slot, h2 * HALF_PAGES + pg] * cpt_ref[...] for h2 in range(nhalf)], axis=0)
            xs.append(x.astype(MXU_DTYPE))
        summ_scr[...] += _dot(jnp.concatenate(xs, axis=1), gsum_ref[it])
        return 0

    lax.fori_loop(0, HALF_PAGES // CMP_PAGES_PER_STEP, cmp_body, 0)
    summ = jnp.concatenate([summ_scr[h2 * 2 * KV_W:(h2 + 1) * 2 * KV_W] for h2 in range(nhalf)], axis=1)
    ckv = _dot(wct_ref[...], summ.astype(MXU_DTYPE))
    kc = ckv[0:KV_W]
    ckt = (kc * cc_ref[...] + pltpu.roll(kc, ROT_HALF, 0) * cs1_ref[...]
           + pltpu.roll(kc, KV_W - ROT_HALF, 0) * cs2_ref[...]).astype(MXU_DTYPE)
    cvt = ckv[KV_W:2 * KV_W].astype(MXU_DTYPE)

    ckn, cvn = new_rows(ckv0_ref[0])
    p1, p2 = joint_softmax(_dot(qbd, ckt), cmask_ref[...] > 0.5, _dot_nt(qbd, ckn), cmaskn_ref[...] > 0.5)
    o_c = _dot_nt(p1.astype(MXU_DTYPE), cvt) + _dot(p2.astype(MXU_DTYPE), cvn)
    pc = p1[0:ngt]
    for h in range(1, HPG):
        pc = pc + p1[h * ngt:(h + 1) * ngt]
    ntile = pc.shape[1] // LANES
    pair = [pc[:, t * LANES:(t + 1) * LANES] + pltpu.roll(pc[:, t * LANES:(t + 1) * LANES], LANES - HALF_PAGES, 1)
            for t in range(ntile)]
    pair = jnp.concatenate([pr[:, HALF_PAGES:2 * HALF_PAGES] for pr in pair]
                           + [pr[:, 3 * HALF_PAGES:LANES] for pr in pair], axis=1)

    cand = cand_ref[...] > 0.5
    sc = jnp.where(cand, pair, -1.0)
    jl = jidx_ref[...]
    rank = jnp.zeros(sc.shape, F32)
    for i in range(sc.shape[1]):
        col = sc[:, i:i + 1]
        beats = (col > sc) | ((col == sc) & (jl > int(jidx_np[i])))
        rank = rank + jnp.where(beats, 1.0, 0.0)
    sel_eo = jnp.where((cand & (rank < n_top)) | (forced_ref[...] > 0.5), 1.0, 0.0).astype(MXU_DTYPE)

    def score_body(ch, mel):
        keep = _dot(sel_eo, eeo_ref[ch]) > 0.5
        bias = jnp.where(keep, 0.0, NEG)
        bias = jnp.concatenate([bias] * HPG, axis=0)
        for pp in range(0, PAGE_CHUNK, 2):
            pg = ch * PAGE_CHUNK + pp
            kt = jnp.concatenate([sbuf[slot, pg, 0:KV_W, :], sbuf[slot, pg + 1, 0:KV_W, :]], axis=1)
            sv = _dot(qbd, kt.astype(MXU_DTYPE)) + bias[:, pp * LANES:(pp + 2) * LANES]
            s_scr[pg] = sv[:, 0:LANES]
            s_scr[pg + 1] = sv[:, LANES:2 * LANES]
            mel = jnp.maximum(mel, jnp.maximum(sv[:, 0:LANES], sv[:, LANES:2 * LANES]))
        return mel

    mel = lax.fori_loop(0, npages // PAGE_CHUNK, score_body, jnp.full((nrow, LANES), NEG, F32))
    ksn, vsn = new_rows(kvs_ref[0])
    nok = nmask_ref[...] > 0.5
    s_new = jnp.where(nok, _dot_nt(qbd, ksn), NEG)
    m = jnp.maximum(jnp.max(mel, axis=1, keepdims=True), jnp.max(s_new, axis=1, keepdims=True))
    mb = jnp.broadcast_to(m, (nrow, LANES))

    def value_body(ch, carry):
        lel, acc = carry
        for pp in range(0, PAGE_CHUNK, 2):
            pg = ch * PAGE_CHUNK + pp
            pv0 = jnp.exp(s_scr[pg] - mb)
            pv1 = jnp.exp(s_scr[pg + 1] - mb)
            lel = lel + (pv0 + pv1)
            vt = jnp.concatenate([sbuf[slot, pg, KV_W:2 * KV_W, :], sbuf[slot, pg + 1, KV_W:2 * KV_W, :]], axis=1)
            acc = acc + _dot_nt(jnp.concatenate([pv0, pv1], axis=1).astype(MXU_DTYPE), vt.astype(MXU_DTYPE))
        return lel, acc

    e_new = jnp.where(nok, jnp.exp(s_new - m), 0.0)
    lel, acc = lax.fori_loop(0, npages // PAGE_CHUNK, value_body,
                             (jnp.zeros((nrow, LANES), F32), _dot(e_new.astype(MXU_DTYPE), vsn)))
    l = jnp.sum(lel, axis=1, keepdims=True) + jnp.sum(e_new, axis=1, keepdims=True)
    o_s = acc / jnp.maximum(l, 1e-30)

    gs = jax.nn.sigmoid(gt_ref[0])
    outs = [None] * N_HEADS
    for h in range(HPG):
        for g in range(KV_HEADS):
            hh = g * HPG + h
            rs = slice((h * KV_HEADS + g) * t_new, (h * KV_HEADS + g + 1) * t_new)
            cs = slice(g * HEAD_DIM, (g + 1) * HEAD_DIM)
            col = hh * N_BRANCH
            outs[hh] = (gs[:, col:col + 1] * o_c[rs, cs] + gs[:, col + 1:col + 2] * o_s[rs, cs]
                        + gs[:, col + 2:col + 3] * o_w[rs, cs])
    o_ref[0] = jnp.concatenate(outs, axis=1)


def _dims_major(a):
    n, toks = a.shape[:2]
    return jnp.transpose(a, (0, 2, 3, 4, 1)).reshape(n, 2 * KV_W, toks)


def _attn_sample(q, gates, kvs, kvw, ckv0, cache_cmp, cache_sel, cache_win, page_table, wts, t_new):
    nseq, npages = page_table.shape
    past = npages * PAGE_SIZE
    wb = cache_win.shape[1]
    nrow = HPG * KV_HEADS * t_new
    ngt = KV_HEADS * t_new
    nlane = npages * CMP_LANE_BLKS
    assert t_new <= L_CMP and past % L_SEL == 0 and wb == WINDOW and wb % LANES == 0
    assert npages % HALF_PAGES == 0 and npages % PAGE_CHUNK == 0 and 2 * npages == LANES
    tb = _sample_tables(past, t_new, wb)
    cc, cs1, cs2 = _rope_tables_t((tb["blk"] + 1) * L_CMP - 1)
    cpt = jnp.tile(wts["cp"].T, (1, CMP_LANE_BLKS))
    seq3 = lambda r, w: pl.BlockSpec((1, r, w), lambda b, pt: (b, 0, 0))
    const = lambda *s: pl.BlockSpec(s, lambda b, pt: (0,) * len(s), pipeline_mode=pl.Buffered(1))
    hbm = pl.BlockSpec(memory_space=pl.ANY)
    in_specs = [seq3(t_new, ATT_W), seq3(t_new, LANES), seq3(t_new, 2 * KV_W), seq3(t_new, 2 * KV_W),
                seq3(1, 2 * KV_W), seq3(2 * KV_W, wb), hbm, hbm,
                const(2 * KV_W, PAGE_SIZE), const(2 * KV_W, 2 * KV_W), const(KV_W, nlane), const(KV_W, nlane),
                const(KV_W, nlane), const(nrow, nlane), const(nrow, 2 * t_new), const(ngt, 2 * npages),
                const(ngt, 2 * npages), const(nrow, 2 * t_new), const(nrow, wb), const(nrow, 2 * t_new),
                const(1, 2 * npages), const(npages // PAGE_CHUNK, 2 * npages, PAGE_CHUNK * PAGE_SIZE),
                const(HALF_PAGES // CMP_PAGES_PER_STEP, CMP_PAGES_PER_STEP * PAGE_SIZE, LANES)]
    grid_spec = pltpu.PrefetchScalarGridSpec(
        num_scalar_prefetch=1, grid=(nseq,), in_specs=in_specs,
        out_specs=(seq3(t_new, ATT_W), seq3(2 * KV_W, wb)),
        scratch_shapes=[pltpu.VMEM((2, npages, 2 * KV_W, PAGE_SIZE), F32),
                        pltpu.VMEM((2, npages, 2 * KV_W, PAGE_SIZE), F32),
                        pltpu.SemaphoreType.DMA((2, 2)),
                        pltpu.VMEM((nlane // LANES * 2 * KV_W, LANES), F32),
                        pltpu.VMEM((npages, nrow, PAGE_SIZE), F32)])
    return pl.pallas_call(
        functools.partial(_attn_sample_kernel, jidx_np=tb["jidx_np"], n_top=N_TOP),
        grid_spec=grid_spec,
        out_shape=(jax.ShapeDtypeStruct((nseq, t_new, ATT_W), F32),
                   jax.ShapeDtypeStruct((nseq, 2 * KV_W, wb), F32)),
        compiler_params=pltpu.CompilerParams(dimension_semantics=("arbitrary",),
                                             vmem_limit_bytes=SAMPLE_VMEM_LIMIT),
    )(page_table, q.reshape(nseq, t_new, ATT_W), gates.reshape(nseq, t_new, LANES),
      kvs.reshape(nseq, t_new, 2 * KV_W), kvw.reshape(nseq, t_new, 2 * KV_W), ckv0.reshape(nseq, 1, 2 * KV_W),
      _dims_major(cache_win), _dims_major(cache_cmp), _dims_major(cache_sel), cpt, wts["wc"].T, cc, cs1, cs2,
      tb["cmask"], tb["cmask_new"], tb["cand"], tb["forced"], tb["nmask"], tb["wmask"], tb["wmask_new"],
      tb["jidx"], tb["eeo"], tb["gsum"])


def _sample_group(x_sample, cache_cmp, cache_sel, cache_win, state_pool, page_table, g_pre_mix, g_post_mix,
                  g_pre_ffn, g_post_ffn, wts):
    nseq, t_new, _ = x_sample.shape
    past = page_table.shape[1] * PAGE_SIZE
    wb = cache_win.shape[1]
    x2 = x_sample.reshape(nseq * t_new, D_MODEL)
    hist16 = jnp.pad(state_pool, ((0, 0), (HIST_ROWS - POOL_HIST, 0), (0, 0)))
    q, kvc, kvs, kvw, gates, pool_y, znew, ckv0 = _inproj_sample(x2, g_pre_mix, wts, hist16, nseq, t_new, past)
    o_att, win_t = _attn_sample(q, gates, kvs, kvw, ckv0, cache_cmp, cache_sel, cache_win, page_table, wts, t_new)
    y = _post(x2, o_att.reshape(nseq * t_new, ATT_W), pool_y, g_post_mix, g_pre_ffn, g_post_ffn, wts)
    kv6 = lambda a: a.reshape(1, nseq, t_new, 2, KV_HEADS, HEAD_DIM)
    win_new = jnp.transpose(win_t.reshape(1, nseq, 2, KV_HEADS, HEAD_DIM, wb), (0, 1, 5, 2, 3, 4))
    return (y.reshape(nseq, t_new, D_MODEL), kv6(kvc), kv6(kvs), win_new, znew[None, :, 1:, :])


def kernel(x_prompt, x_sample, cache_kv_cmp, cache_kv_sel, cache_kv_win, state_pool, page_table, g_pre_mix,
           g_post_mix, g_pre_ffn, g_post_ffn, w_in, cmp_pos, w_cmp, w_pool, pool_scale, w_out, w_gate, w_up, w_down):
    wts = _prep_weights(w_in[0], cmp_pos[0], w_cmp[0], w_pool[0], pool_scale[0], w_out[0], w_gate[0], w_up[0],
                        w_down[0])
    yp, cmp_p, sel_p, win_p, pool_p = _prompt_group(x_prompt, g_pre_mix[0], g_post_mix[0], g_pre_ffn[0],
                                                    g_post_ffn[0], wts)
    ys, cmp_s, sel_s, win_s, pool_s = _sample_group(x_sample, cache_kv_cmp[0], cache_kv_sel[0], cache_kv_win[0],
                                                    state_pool[0], page_table, g_pre_mix[0], g_post_mix[0],
                                                    g_pre_ffn[0], g_post_ffn[0], wts)
    return (yp, ys, cmp_p, sel_p, win_p, pool_p, cmp_s, sel_s, win_s, pool_s)
```

```python
import functools

import jax
import jax.numpy as jnp
import numpy as np
from jax import lax
from jax.experimental import pallas as pl
from jax.experimental.pallas import tpu as pltpu

F32 = jnp.float32
MXU_DTYPE = jnp.bfloat16

D_MODEL = 1024
PAGE_SIZE = 128
N_HEADS = 8
KV_HEADS = 2
HPG = N_HEADS // KV_HEADS
HEAD_DIM = 64
ATT_W = N_HEADS * HEAD_DIM
KV_W = KV_HEADS * HEAD_DIM
N_BRANCH = 3
L_CMP = 32
L_SEL = 64
N_SEL = 16
N_FORCED = 3
N_TOP = N_SEL - N_FORCED
WINDOW = 512
SCALE = HEAD_DIM ** -0.5
ROT_DIM = HEAD_DIM // 4
ROT_HALF = ROT_DIM // 2
ROPE_THETA = 500000.0
POOL_WINDOWS = (2, 4, 8, 16)
POOL_W = D_MODEL - ATT_W
POOL_GC = POOL_W // len(POOL_WINDOWS)
POOL_HIST = max(POOL_WINDOWS) - 1
HIST_ROWS = POOL_HIST + 1
N_GATE = N_HEADS * N_BRANCH
D_FF = -(-8 * D_MODEL // (3 * 256)) * 256
EPS = 1e-6
NEG = -1e30

LANES = 128
C_Q = 0
C_KVC = C_Q + ATT_W
C_KVS = C_KVC + 2 * KV_W
C_KVW = C_KVS + 2 * KV_W
C_POOL = C_KVW + 2 * KV_W
C_GATE = C_POOL + POOL_W
IN_W_PAD = C_GATE + LANES

TM_IN = 1024
TM_POST = 1024
Q_BLK = 256
K_CHUNK = 512
FF_CHUNKS = 11
VMEM_LIMIT = 56 * 1024 * 1024
SAMPLE_VMEM_LIMIT = 60 * 1024 * 1024


def _const_spec(shape):
    nd = len(shape)
    return pl.BlockSpec(shape, lambda *_: (0,) * nd, pipeline_mode=pl.Buffered(1))


def _rms(x, g):
    return x * lax.rsqrt(jnp.mean(x * x, axis=-1, keepdims=True) + EPS) * g


def _rope128(x, c, s1, s2):
    return x * c + pltpu.roll(x, ROT_HALF, 1) * s1 + pltpu.roll(x, LANES - ROT_HALF, 1) * s2


def _dot(a, b):
    return jnp.dot(a, b, preferred_element_type=F32)


def _dot_nt(a, b):
    return lax.dot_general(a, b, (((1,), (1,)), ((), ())), preferred_element_type=F32)


def _dot_tn(a, b):
    return lax.dot_general(a, b, (((0,), (0,)), ((), ())), preferred_element_type=F32)


def _rope_tables(pos):
    pos = jnp.asarray(np.asarray(pos), jnp.int32)
    inv = ROPE_THETA ** (-jnp.arange(ROT_HALF, dtype=F32) * 2.0 / ROT_DIM)
    ang = pos.astype(F32)[:, None] * inv
    cos, sin = jnp.cos(ang), jnp.sin(ang)
    n = pos.shape[0]
    one = jnp.ones((n, HEAD_DIM - ROT_DIM), F32)
    zero = jnp.zeros((n, HEAD_DIM - ROT_DIM), F32)
    zh = jnp.zeros((n, ROT_HALF), F32)
    c = jnp.concatenate([cos, cos, one], axis=1)
    s1 = jnp.concatenate([zh, sin, zero], axis=1)
    s2 = jnp.concatenate([-sin, zh, zero], axis=1)
    tile = lambda t: jnp.concatenate([t, t], axis=1)
    return tile(c), tile(s1), tile(s2)


def _prep_weights(w_in, cmp_pos, w_cmp, w_pool, pool_scale, w_out, w_gate, w_up, w_down):
    w_in_r = jnp.concatenate([
        w_in[:, :ATT_W + 6 * KV_W],
        w_in[:, ATT_W + 6 * KV_W + N_GATE:],
        w_in[:, ATT_W + 6 * KV_W:ATT_W + 6 * KV_W + N_GATE],
        jnp.zeros((D_MODEL, LANES - N_GATE), w_in.dtype)], axis=1).astype(MXU_DTYPE)
    cp = jnp.concatenate([cmp_pos[:, 0], cmp_pos[:, 0], cmp_pos[:, 1], cmp_pos[:, 1]], axis=1)
    wc = jnp.zeros((2 * KV_W, 2 * KV_W), F32)
    for i, c in enumerate((0, 0, 1, 1)):
        wc = wc.at[i * HEAD_DIM:(i + 1) * HEAD_DIM, i * HEAD_DIM:(i + 1) * HEAD_DIM].set(w_cmp[c])
    fc = D_FF // FF_CHUNKS
    return dict(
        w_in=w_in_r, cp=cp, wc=wc.astype(MXU_DTYPE), w_pool=w_pool.astype(MXU_DTYPE),
        pool_scale=pool_scale.reshape(1, POOL_W), w_out=w_out.astype(MXU_DTYPE),
        w_gate=w_gate.reshape(D_MODEL, FF_CHUNKS, fc).transpose(1, 0, 2).astype(MXU_DTYPE),
        w_up=w_up.reshape(D_MODEL, FF_CHUNKS, fc).transpose(1, 0, 2).astype(MXU_DTYPE),
        w_down=w_down.reshape(FF_CHUNKS, fc, D_MODEL).astype(MXU_DTYPE))


def _project_rows(x, g, w_ref, rc, rs1, rs2):
    h = _rms(x, g).astype(MXU_DTYPE)
    z = _dot(h, w_ref[...])
    q = jnp.concatenate(
        [_rope128(z[:, C_Q + i * LANES:C_Q + (i + 1) * LANES], rc, rs1, rs2) for i in range(ATT_W // LANES)],
        axis=1) * SCALE
    kvc = z[:, C_KVC:C_KVS]
    kvs = jnp.concatenate([_rope128(z[:, C_KVS:C_KVS + KV_W], rc, rs1, rs2), z[:, C_KVS + KV_W:C_KVW]], axis=1)
    kvw = jnp.concatenate([_rope128(z[:, C_KVW:C_KVW + KV_W], rc, rs1, rs2), z[:, C_KVW + KV_W:C_POOL]], axis=1)
    u = z[:, C_POOL:C_GATE]
    gates = z[:, C_GATE:IN_W_PAD]
    return q, kvc, kvs, kvw, u, gates


def _compress_rows(kvc, cp, wc_ref, cc, cs1, cs2):
    n = kvc.shape[0] // L_CMP
    summ = jnp.sum(kvc.reshape(n, L_CMP, 2 * KV_W) * cp[None], axis=1)
    ckv = _dot(summ.astype(MXU_DTYPE), wc_ref[...])
    return jnp.concatenate([_rope128(ckv[:, :KV_W], cc, cs1, cs2), ckv[:, KV_W:]], axis=1)


def _pool_windows(zf, rows, cnt):
    outs = []
    for gi, w in enumerate(POOL_WINDOWS):
        a = zf[:, gi * POOL_GC:(gi + 1) * POOL_GC]
        s, sh = a, 1
        while sh < w:
            s = s + pltpu.roll(s, sh, 0)
            sh *= 2
        outs.append(rows(s) / cnt[gi] - rows(a))
    return outs


def _pool_project(d_list, wp_ref, ps):
    y = jnp.concatenate([_dot(d.astype(MXU_DTYPE), wp_ref[gi]) for gi, d in enumerate(d_list)], axis=1)
    return y * ps


def _inproj_prompt_kernel(x_ref, g_ref, w_ref, rc_ref, rs1_ref, rs2_ref, cp_ref, wc_ref, cc_ref, cs1_ref, cs2_ref,
                          wp_ref, ps_ref,
                          q_ref, kvc_ref, kvs_ref, kvw_ref, kvsb_ref, kvwb_ref, gates_ref, pool_ref, ulast_ref,
                          ckv_ref, z_scr):
    j = pl.program_id(1)
    tm = x_ref.shape[0]
    nblk = tm // L_CMP
    r0 = pl.multiple_of(j * tm, tm)
    rows = pl.ds(r0, tm)
    q, kvc, kvs, kvw, u, gates = _project_rows(x_ref[...], g_ref[...], w_ref, rc_ref[rows, :], rs1_ref[rows, :],
                                               rs2_ref[rows, :])
    q_ref[...] = q.astype(q_ref.dtype)
    kvc_ref[0] = kvc.T
    kvs_ref[0] = kvs.T
    kvw_ref[0] = kvw.T
    kvsb_ref[...] = kvs.astype(kvsb_ref.dtype)
    kvwb_ref[...] = kvw.astype(kvwb_ref.dtype)
    gates_ref[...] = gates

    crow = pl.ds(pl.multiple_of(j * nblk, nblk), nblk)
    ckv_ref[...] = _compress_rows(kvc, cp_ref[...], wc_ref, cc_ref[crow, :], cs1_ref[crow, :], cs2_ref[crow, :])

    @pl.when(j == 0)
    def _():
        z_scr[0:HIST_ROWS, :] = jnp.zeros((HIST_ROWS, POOL_W), F32)

    @pl.when(j > 0)
    def _():
        z_scr[0:HIST_ROWS, :] = z_scr[tm:tm + HIST_ROWS, :]

    z_scr[HIST_ROWS:, :] = u
    pos = (r0 + lax.broadcasted_iota(jnp.int32, (tm, 1), 0)).astype(F32)
    cnt = [jnp.minimum(pos + 1.0, float(w)) for w in POOL_WINDOWS]
    d_list = _pool_windows(z_scr[...], lambda a: a[HIST_ROWS:, :], cnt)
    pool_ref[...] = _pool_project(d_list, wp_ref, ps_ref[...]).astype(pool_ref.dtype)
    ulast_ref[0] = u[tm - HIST_ROWS:, :]


def _inproj_prompt(x2, g_pre, wts, seq, batch):
    n = x2.shape[0]
    tm = min(TM_IN, seq)
    nj = seq // tm
    nblk = tm // L_CMP
    rc, rs1, rs2 = _rope_tables(np.arange(seq))
    cc, cs1, cs2 = _rope_tables((np.arange(seq // L_CMP) + 1) * L_CMP - 1)
    row = lambda w: pl.BlockSpec((tm, w), lambda b, j: (b * nj + j, 0))
    out_shape = (
        jax.ShapeDtypeStruct((n, ATT_W), MXU_DTYPE),
        jax.ShapeDtypeStruct((batch, 2 * KV_W, seq), F32),
        jax.ShapeDtypeStruct((batch, 2 * KV_W, seq), F32),
        jax.ShapeDtypeStruct((batch, 2 * KV_W, seq), F32),
        jax.ShapeDtypeStruct((n, 2 * KV_W), MXU_DTYPE),
        jax.ShapeDtypeStruct((n, 2 * KV_W), MXU_DTYPE),
        jax.ShapeDtypeStruct((n, LANES), F32),
        jax.ShapeDtypeStruct((n, POOL_W), MXU_DTYPE),
        jax.ShapeDtypeStruct((batch, HIST_ROWS, POOL_W), F32),
        jax.ShapeDtypeStruct((n // L_CMP, 2 * KV_W), F32),
    )
    dims_major = pl.BlockSpec((1, 2 * KV_W, tm), lambda b, j: (b, 0, j))
    out_specs = (row(ATT_W), dims_major, dims_major, dims_major, row(2 * KV_W), row(2 * KV_W), row(LANES),
                 row(POOL_W), pl.BlockSpec((1, HIST_ROWS, POOL_W), lambda b, j: (b, 0, 0)),
                 pl.BlockSpec((nblk, 2 * KV_W), lambda b, j: (b * nj + j, 0)))
    in_specs = [row(D_MODEL), _const_spec((1, D_MODEL)), _const_spec((D_MODEL, IN_W_PAD)),
                _const_spec((seq, LANES)), _const_spec((seq, LANES)), _const_spec((seq, LANES)),
                _const_spec((L_CMP, 2 * KV_W)), _const_spec((2 * KV_W, 2 * KV_W)),
                _const_spec((seq // L_CMP, LANES)), _const_spec((seq // L_CMP, LANES)),
                _const_spec((seq // L_CMP, LANES)),
                _const_spec((len(POOL_WINDOWS), POOL_GC, POOL_GC)), _const_spec((1, POOL_W))]
    return pl.pallas_call(
        _inproj_prompt_kernel,
        grid=(batch, nj),
        in_specs=in_specs, out_specs=out_specs, out_shape=out_shape,
        scratch_shapes=[pltpu.VMEM((tm + HIST_ROWS, POOL_W), F32)],
        compiler_params=pltpu.CompilerParams(dimension_semantics=("arbitrary", "arbitrary"),
                                             vmem_limit_bytes=VMEM_LIMIT),
    )(x2, g_pre.reshape(1, D_MODEL), wts["w_in"], rc, rs1, rs2, wts["cp"], wts["wc"], cc, cs1, cs2,
      wts["w_pool"], wts["pool_scale"])


def _post_kernel(x_ref, o_ref, p_ref, wo_ref, gpm_ref, gpf_ref, gqf_ref, wg_ref, wu_ref, wd_ref, y_ref):
    mixed = _dot(jnp.concatenate([o_ref[...].astype(MXU_DTYPE), p_ref[...]], axis=1), wo_ref[...])
    x1 = x_ref[...] + _rms(mixed, gpm_ref[...])
    h2 = _rms(x1, gpf_ref[...]).astype(MXU_DTYPE)
    acc = None
    for c in range(FF_CHUNKS):
        f = (jax.nn.silu(_dot(h2, wg_ref[c])) * _dot(h2, wu_ref[c])).astype(MXU_DTYPE)
        part = _dot(f, wd_ref[c])
        acc = part if acc is None else acc + part
    y_ref[...] = x1 + _rms(acc, gqf_ref[...])


def _post(x2, o_att, pool_y, g_post_mix, g_pre_ffn, g_post_ffn, wts):
    n = x2.shape[0]
    tm = min(TM_POST, n)
    fc = D_FF // FF_CHUNKS
    row = lambda w: pl.BlockSpec((tm, w), lambda i: (i, 0))
    return pl.pallas_call(
        _post_kernel,
        grid=(n // tm,),
        in_specs=[row(D_MODEL), row(ATT_W), row(POOL_W), _const_spec((D_MODEL, D_MODEL)),
                  _const_spec((1, D_MODEL)), _const_spec((1, D_MODEL)), _const_spec((1, D_MODEL)),
                  _const_spec((FF_CHUNKS, D_MODEL, fc)), _const_spec((FF_CHUNKS, D_MODEL, fc)),
                  _const_spec((FF_CHUNKS, fc, D_MODEL))],
        out_specs=row(D_MODEL),
        out_shape=jax.ShapeDtypeStruct((n, D_MODEL), F32),
        compiler_params=pltpu.CompilerParams(dimension_semantics=("arbitrary",), vmem_limit_bytes=VMEM_LIMIT),
    )(x2, o_att, pool_y, wts["w_out"], g_post_mix.reshape(1, D_MODEL), g_pre_ffn.reshape(1, D_MODEL),
      g_post_ffn.reshape(1, D_MODEL), wts["w_gate"], wts["w_up"], wts["w_down"])


def _unselected_blocks_t(p_slc_t, c_row, n_top):
    nb = p_slc_t.shape[0]
    jj = lax.broadcasted_iota(jnp.int32, p_slc_t.shape, 0)
    cand = (jj >= 1) & (jj <= c_row - 2)
    sc = jnp.where(cand, p_slc_t, -1.0)
    rank = jnp.zeros(sc.shape, F32)
    for i in range(nb):
        row = sc[i:i + 1, :]
        beats = (row > sc) | ((row == sc) & (jj > i))
        rank = rank + jnp.where(beats, 1.0, 0.0)
    sel = (cand & (rank < n_top)) | (jj == 0) | (jj == c_row) | (jj == c_row - 1)
    return jnp.where(sel, 0.0, 1.0)


def _lane_tiles(a, n):
    return jnp.concatenate([a] * n, axis=1)


def _attn_prompt_kernel(q_ref, gt_ref, ckv_ref, ks_ref, kw_ref, eneg_ref, o_ref,
                        qaug_scr, oc_scr, s_scr, mel_scr, mb_scr, lel_scr, acc_scr):
    qb = pl.program_id(1)
    qblk = q_ref.shape[0]
    rows = HPG * qblk
    ncb = ckv_ref.shape[1]
    half = ncb // 2
    wspan = WINDOW + qblk
    start = qb * qblk
    tpos_c = start + lax.broadcasted_iota(jnp.int32, (qblk, 1), 0)
    tpos_r = start + lax.broadcasted_iota(jnp.int32, (1, qblk), 1)
    gs = jax.nn.sigmoid(gt_ref[...])
    ckv = ckv_ref[0]
    ck = ckv[:, :KV_W].astype(MXU_DTYPE)
    cv = ckv[:, KV_W:].astype(MXU_DTYPE)
    crow = lax.broadcasted_iota(jnp.int32, (ncb, rows), 0)
    cblk = 2 * (crow % half) + crow // half
    cmask = ((cblk + 1) * L_CMP - 1) <= _lane_tiles(tpos_r, HPG)

    ws = pl.multiple_of(jnp.maximum(start - WINDOW, 0), qblk)
    dwin = tpos_c - (ws + lax.broadcasted_iota(jnp.int32, (1, wspan), 1))
    bias_w = jnp.where((dwin >= 0) & (dwin < WINDOW), 0.0, NEG)
    n_chunks = start // K_CHUNK + 1
    lane = lax.broadcasted_iota(jnp.int32, (qblk, LANES), 1)

    outs = [None] * N_HEADS
    for g in range(KV_HEADS):
        pieces = []
        for h in range(HPG):
            hh = g * HPG + h
            tile = q_ref[:, (hh // 2) * LANES:(hh // 2 + 1) * LANES].astype(F32)
            if hh % 2 != g:
                tile = pltpu.roll(tile, HEAD_DIM, 1)
            pieces.append(jnp.where((lane >= g * HEAD_DIM) & (lane < (g + 1) * HEAD_DIM), tile, 0.0))
        qg = jnp.concatenate(pieces, axis=0).astype(MXU_DTYPE)

        s = jnp.where(cmask, _dot_nt(ck, qg), NEG)
        e = jnp.where(cmask, jnp.exp(s - jnp.max(s, axis=0, keepdims=True)), 0.0)
        p = e / jnp.maximum(jnp.sum(e, axis=0, keepdims=True), 1e-30)
        o_c = _dot_tn(p.astype(MXU_DTYPE), cv)
        pc = p[:, 0:qblk]
        for h in range(1, HPG):
            pc = pc + p[:, h * qblk:(h + 1) * qblk]
        unsel = _unselected_blocks_t(pc[0:half] + pc[half:ncb], tpos_r // L_SEL, N_TOP)
        unsel = jnp.concatenate([unsel, jnp.zeros((LANES - half, qblk), F32)], axis=0)
        unsel = unsel.T.astype(MXU_DTYPE)
        gr = slice(g * rows, (g + 1) * rows)
        qaug_scr[gr, :] = jnp.concatenate([qg, jnp.concatenate([unsel] * HPG, axis=0)], axis=1)
        oc_scr[gr, :] = o_c

    mel_scr[...] = jnp.full((KV_HEADS * rows, LANES), NEG, F32)

    def score_chunk(ci, causal):
        k0 = pl.multiple_of(ci * K_CHUNK, K_CHUNK)
        kaug = jnp.concatenate([ks_ref[0, pl.ds(k0, K_CHUNK), 0:KV_W], eneg_ref[pl.ds(k0, K_CHUNK), :]], axis=1)
        sv = _dot_nt(qaug_scr[...], kaug)
        if causal:
            kpos = k0 + lax.broadcasted_iota(jnp.int32, (1, K_CHUNK), 1)
            bias = jnp.where(kpos <= tpos_c, 0.0, NEG)
            sv = jnp.concatenate([sv[r * qblk:(r + 1) * qblk] + bias for r in range(KV_HEADS * HPG)], axis=0)
        s_scr[ci] = sv
        mh = sv[:, 0:LANES]
        for j in range(1, K_CHUNK // LANES):
            mh = jnp.maximum(mh, sv[:, j * LANES:(j + 1) * LANES])
        mel_scr[...] = jnp.maximum(mel_scr[...], mh)

    def score_body(ci, _):
        score_chunk(ci, False)
        return 0

    lax.fori_loop(0, n_chunks - 1, score_body, 0)
    score_chunk(n_chunks - 1, True)
    mb_scr[...] = jnp.broadcast_to(jnp.max(mel_scr[...], axis=1, keepdims=True), (KV_HEADS * rows, LANES))
    lel_scr[...] = jnp.zeros((KV_HEADS * rows, LANES), F32)
    acc_scr[...] = jnp.zeros((KV_HEADS * rows, LANES), F32)

    def value_body(ci, _):
        k0 = pl.multiple_of(ci * K_CHUNK, K_CHUNK)
        pv = jnp.exp(s_scr[ci] - _lane_tiles(mb_scr[...], K_CHUNK // LANES))
        lsum = pv[:, 0:LANES]
        for j in range(1, K_CHUNK // LANES):
            lsum = lsum + pv[:, j * LANES:(j + 1) * LANES]
        lel_scr[...] += lsum
        acc_scr[...] += _dot(pv.astype(MXU_DTYPE), ks_ref[0, pl.ds(k0, K_CHUNK), KV_W:2 * KV_W])
        return 0

    lax.fori_loop(0, n_chunks, value_body, 0)
    acc_scr[...] = acc_scr[...] / jnp.maximum(jnp.sum(lel_scr[...], axis=1, keepdims=True), 1e-30)

    for g in range(KV_HEADS):
        gr = slice(g * rows, (g + 1) * rows)
        o_c = oc_scr[gr, :]
        o_s = acc_scr[gr, :]
        qg = qaug_scr[gr, 0:KV_W]
        sw = _dot_nt(qg, kw_ref[0, pl.ds(ws, wspan), 0:KV_W])
        sw = jnp.concatenate([sw[h * qblk:(h + 1) * qblk] + bias_w for h in range(HPG)], axis=0)
        ew = jnp.exp(sw - jnp.max(sw, axis=1, keepdims=True))
        o_w = (_dot(ew.astype(MXU_DTYPE), kw_ref[0, pl.ds(ws, wspan), KV_W:2 * KV_W])
               / jnp.maximum(jnp.sum(ew, axis=1, keepdims=True), 1e-30))

        for h in range(HPG):
            hh = g * HPG + h
            col = hh * N_BRANCH
            rs = slice(h * qblk, (h + 1) * qblk)
            cs = slice(g * HEAD_DIM, (g + 1) * HEAD_DIM)
            outs[hh] = (gs[:, col:col + 1] * o_c[rs, cs] + gs[:, col + 1:col + 2] * o_s[rs, cs]
                        + gs[:, col + 2:col + 3] * o_w[rs, cs])
    o_ref[...] = jnp.concatenate(outs, axis=1).astype(o_ref.dtype)


def _attn_prompt(q, gates, ckv_perm, kvsb, kvwb, seq, batch):
    n = q.shape[0]
    qblk = min(Q_BLK, seq)
    nqb = seq // qblk
    nb = seq // L_SEL
    nchunk = seq // K_CHUNK
    rows = HPG * qblk
    assert seq >= WINDOW + qblk and seq % K_CHUNK == 0 and K_CHUNK % qblk == 0
    assert nb <= LANES
    eneg = jnp.asarray(NEG * (np.arange(seq)[:, None] // L_SEL == np.arange(LANES)[None, :]), MXU_DTYPE)
    row = lambda w: pl.BlockSpec((qblk, w), lambda b, i: (b * nqb + i, 0))
    per_b = lambda r, w: pl.BlockSpec((1, r, w), lambda b, i: (b, 0, 0))
    return pl.pallas_call(
        _attn_prompt_kernel,
        grid=(batch, nqb),
        in_specs=[row(ATT_W), row(LANES), per_b(seq // L_CMP, 2 * KV_W), per_b(seq, 2 * KV_W),
                  per_b(seq, 2 * KV_W), _const_spec((seq, LANES))],
        out_specs=row(ATT_W),
        out_shape=jax.ShapeDtypeStruct((n, ATT_W), MXU_DTYPE),
        scratch_shapes=[pltpu.VMEM((KV_HEADS * rows, 2 * LANES), MXU_DTYPE),
                        pltpu.VMEM((KV_HEADS * rows, LANES), F32),
                        pltpu.VMEM((nchunk, KV_HEADS * rows, K_CHUNK), F32)]
        + [pltpu.VMEM((KV_HEADS * rows, LANES), F32)] * 4,
        compiler_params=pltpu.CompilerParams(dimension_semantics=("arbitrary", "arbitrary"),
                                             vmem_limit_bytes=VMEM_LIMIT),
    )(q, gates, ckv_perm, kvsb.reshape(batch, seq, 2 * KV_W), kvwb.reshape(batch, seq, 2 * KV_W), eneg)


def _prompt_group(x_prompt, g_pre_mix, g_post_mix, g_pre_ffn, g_post_ffn, wts):
    batch, seq, _ = x_prompt.shape
    x2 = x_prompt.reshape(batch * seq, D_MODEL)
    q, kvc, kvs, kvw, kvsb, kvwb, gates, pool_y, ulast, ckv = _inproj_prompt(x2, g_pre_mix, wts, seq, batch)
    ncb = seq // L_CMP
    ckv_perm = ckv.reshape(batch, ncb // 2, 2, 2 * KV_W).transpose(0, 2, 1, 3).reshape(batch, ncb, 2 * KV_W)
    o_att = _attn_prompt(q, gates, ckv_perm, kvsb, kvwb, seq, batch)
    y = _post(x2, o_att, pool_y, g_post_mix, g_pre_ffn, g_post_ffn, wts)
    wkeep = min(WINDOW, seq)
    kv6 = lambda a: jnp.transpose(a.reshape(1, batch, 2, KV_HEADS, HEAD_DIM, a.shape[-1]), (0, 1, 5, 2, 3, 4))
    return (y.reshape(batch, seq, D_MODEL), kv6(kvc), kv6(kvs), kv6(kvw[:, :, seq - wkeep:]),
            ulast[None, :, 1:, :])


def _inproj_sample_kernel(x_ref, g_ref, w_ref, rc_ref, rs1_ref, rs2_ref, cp_ref, wc_ref, cc_ref, cs1_ref, cs2_ref,
                          wp_ref, ps_ref, hist_ref,
                          q_ref, kvc_ref, kvs_ref, kvw_ref, gates_ref, pool_ref, znew_ref, ckv_ref, z_scr, *, past):
    nseq, zrows, _ = z_scr.shape
    t_new = zrows - HIST_ROWS
    n = nseq * t_new
    q, kvc, kvs, kvw, u, gates = _project_rows(x_ref[...], g_ref[...], w_ref, rc_ref[...], rs1_ref[...],
                                               rs2_ref[...])
    q_ref[...] = q
    kvc_ref[...] = kvc
    kvs_ref[...] = kvs
    kvw_ref[...] = kvw
    gates_ref[...] = gates

    summ = jnp.sum(kvc.reshape(nseq, t_new, 2 * KV_W) * cp_ref[0:t_new, :][None], axis=1)
    ckv = _dot(summ.astype(MXU_DTYPE), wc_ref[...])
    ckv_ref[...] = jnp.concatenate([_rope128(ckv[:, :KV_W], cc_ref[...], cs1_ref[...], cs2_ref[...]),
                                    ckv[:, KV_W:]], axis=1)

    z_scr[:, 0:HIST_ROWS, :] = hist_ref[...]
    z_scr[:, HIST_ROWS:, :] = u.reshape(nseq, t_new, POOL_W)
    zf = z_scr[...].reshape(nseq * zrows, POOL_W)
    tok = lax.broadcasted_iota(jnp.int32, (n, 1), 0) % t_new
    pos = (past + tok).astype(F32)
    cnt = [jnp.minimum(pos + 1.0, float(w)) for w in POOL_WINDOWS]
    take = lambda a: a.reshape(nseq, zrows, POOL_GC)[:, HIST_ROWS:, :].reshape(n, POOL_GC)
    d_list = _pool_windows(zf, take, cnt)
    pool_ref[...] = _pool_project(d_list, wp_ref, ps_ref[...]).astype(pool_ref.dtype)
    znew_ref[...] = z_scr[:, zrows - HIST_ROWS:, :]


def _inproj_sample(x2, g_pre, wts, hist16, nseq, t_new, past):
    n = x2.shape[0]
    rc, rs1, rs2 = _rope_tables(np.tile(past + np.arange(t_new), nseq))
    cc, cs1, cs2 = _rope_tables(np.array([past + L_CMP - 1]))
    full = lambda *s: pl.BlockSpec(s, lambda i: (0,) * len(s))
    out_shape = (
        jax.ShapeDtypeStruct((n, ATT_W), F32),
        jax.ShapeDtypeStruct((n, 2 * KV_W), F32),
        jax.ShapeDtypeStruct((n, 2 * KV_W), F32),
        jax.ShapeDtypeStruct((n, 2 * KV_W), F32),
        jax.ShapeDtypeStruct((n, LANES), F32),
        jax.ShapeDtypeStruct((n, POOL_W), MXU_DTYPE),
        jax.ShapeDtypeStruct((nseq, HIST_ROWS, POOL_W), F32),
        jax.ShapeDtypeStruct((nseq, 2 * KV_W), F32),
    )
    out_specs = (full(n, ATT_W), full(n, 2 * KV_W), full(n, 2 * KV_W), full(n, 2 * KV_W), full(n, LANES),
                 full(n, POOL_W), full(nseq, HIST_ROWS, POOL_W), full(nseq, 2 * KV_W))
    in_specs = [full(n, D_MODEL), full(1, D_MODEL), full(D_MODEL, IN_W_PAD), full(n, LANES), full(n, LANES),
                full(n, LANES), full(L_CMP, 2 * KV_W), full(2 * KV_W, 2 * KV_W), full(1, LANES), full(1, LANES),
                full(1, LANES), full(len(POOL_WINDOWS), POOL_GC, POOL_GC), full(1, POOL_W),
                full(nseq, HIST_ROWS, POOL_W)]
    return pl.pallas_call(
        functools.partial(_inproj_sample_kernel, past=past),
        grid=(1,),
        in_specs=in_specs, out_specs=out_specs, out_shape=out_shape,
        scratch_shapes=[pltpu.VMEM((nseq, HIST_ROWS + t_new, POOL_W), F32)],
        compiler_params=pltpu.CompilerParams(dimension_semantics=("arbitrary",), vmem_limit_bytes=VMEM_LIMIT),
    )(x2, g_pre.reshape(1, D_MODEL), wts["w_in"], rc, rs1, rs2, wts["cp"], wts["wc"], cc, cs1, cs2,
      wts["w_pool"], wts["pool_scale"], hist16)


PAGE_CHUNK = 8
CMP_LANE_BLKS = PAGE_SIZE // L_CMP
HALF_PAGES = LANES // CMP_LANE_BLKS
CMP_PAGES_PER_STEP = 4


def _sample_tables(past, t_new, wb):
    npages = past // PAGE_SIZE
    nrow = HPG * KV_HEADS * t_new
    r = np.arange(nrow)
    t_row = r % t_new
    pos = past + t_row
    lanes = np.arange(npages * CMP_LANE_BLKS)
    page = HALF_PAGES * (lanes // LANES) + lanes % HALF_PAGES
    m = (lanes % LANES) // HALF_PAGES
    blk = CMP_LANE_BLKS * page + (m - 1) % CMP_LANE_BLKS
    cmask = (blk[None, :] + 1) * L_CMP - 1 <= pos[:, None]
    n_new = 2 * t_new
    blk_new = past // L_CMP + np.arange(n_new)
    n_cmp = (past + -(-t_new // L_SEL) * L_SEL) // L_CMP
    cmask_new = ((blk_new[None, :] + 1) * L_CMP - 1 <= pos[:, None]) & (blk_new[None, :] < n_cmp)
    jidx = np.concatenate([2 * np.arange(npages), 2 * np.arange(npages) + 1])
    rg = np.arange(KV_HEADS * t_new)
    c = ((past + rg % t_new) // L_SEL)[:, None]
    jj = jidx[None, :]
    cand = (jj >= 1) & (jj <= c - 2)
    forced = (jj == 0) | (jj == c) | (jj == c - 1)
    l = np.arange(n_new)
    new_blk = past // L_SEL
    c_row = (pos // L_SEL)[:, None]
    new_forced = (new_blk == 0) | (new_blk == c_row) | (new_blk == c_row - 1)
    nmask = new_forced & (l[None, :] <= t_row[:, None]) & (l[None, :] < t_new)
    i = np.arange(wb)
    d = t_row[:, None] + wb - i[None, :]
    wmask = (d >= 0) & (d < WINDOW) & (past - wb + i[None, :] >= 0)
    dn = t_row[:, None] - l[None, :]
    wmask_new = (dn >= 0) & (dn < WINDOW) & (l[None, :] < t_new)
    ck = PAGE_CHUNK * PAGE_SIZE
    k = np.arange(ck)
    eeo = np.zeros((npages // PAGE_CHUNK, 2 * npages, ck), np.float32)
    for ch in range(npages // PAGE_CHUNK):
        pg = PAGE_CHUNK * ch + k // PAGE_SIZE
        upper = (k % PAGE_SIZE) >= L_SEL
        eeo[ch, pg[~upper], k[~upper]] = 1.0
        eeo[ch, npages + pg[upper], k[upper]] = 1.0
    tt = np.arange(PAGE_SIZE)
    steps = HALF_PAGES // CMP_PAGES_PER_STEP
    gsum = np.zeros((steps, CMP_PAGES_PER_STEP * PAGE_SIZE, LANES), np.float32)
    for it in range(steps):
        for kk in range(CMP_PAGES_PER_STEP):
            col = HALF_PAGES * ((tt // L_CMP + 1) % CMP_LANE_BLKS) + it * CMP_PAGES_PER_STEP + kk
            gsum[it, kk * PAGE_SIZE + tt, col] = 1.0
    f = lambda a: jnp.asarray(np.asarray(a, np.float32))
    return dict(gsum=jnp.asarray(gsum, MXU_DTYPE), cmask=f(cmask), cmask_new=f(cmask_new), cand=f(cand), forced=f(forced), nmask=f(nmask),
                wmask=f(wmask), wmask_new=f(wmask_new), jidx=jnp.asarray(jidx[None, :], jnp.int32),
                eeo=jnp.asarray(eeo, MXU_DTYPE), blk=blk, jidx_np=jidx)


def _rope_tables_t(pos):
    c, s1, s2 = _rope_tables(pos)
    return c.T, s1.T, s2.T


def _attn_sample_kernel(pt_ref, q_ref, gt_ref, kvs_ref, kvw_ref, ckv0_ref, cw_ref, cmp_hbm, sel_hbm,
                        cpt_ref, wct_ref, cc_ref, cs1_ref, cs2_ref, cmask_ref, cmaskn_ref, cand_ref, forced_ref,
                        nmask_ref, wmask_ref, wmaskn_ref, jidx_ref, eeo_ref, gsum_ref,
                        o_ref, wout_ref,
                        cbuf, sbuf, sem, summ_scr, s_scr, *, jidx_np, n_top):
    b = pl.program_id(0)
    nseq = pl.num_programs(0)
    npages = pt_ref.shape[1]
    t_new = q_ref.shape[1]
    nrow = HPG * KV_HEADS * t_new
    ngt = KV_HEADS * t_new
    slot = b % 2

    def copies(seq, sl):
        out = []
        for p in range(npages):
            pg = pt_ref[seq, p]
            out.append(pltpu.make_async_copy(cmp_hbm.at[pg], cbuf.at[sl, p], sem.at[0, sl]))
            out.append(pltpu.make_async_copy(sel_hbm.at[pg], sbuf.at[sl, p], sem.at[1, sl]))
        return out

    @pl.when(b == 0)
    def _():
        for cpy in copies(b, slot):
            cpy.start()

    @pl.when(b + 1 < nseq)
    def _():
        for cpy in copies(b + 1, 1 - slot):
            cpy.start()

    qb = q_ref[0]
    zeros = jnp.zeros((t_new, HEAD_DIM), F32)
    blocks = []
    for h in range(HPG):
        for g in range(KV_HEADS):
            piece = qb[:, (g * HPG + h) * HEAD_DIM:(g * HPG + h + 1) * HEAD_DIM]
            blocks.append(jnp.concatenate([piece, zeros] if g == 0 else [zeros, piece], axis=1))
    qbd = jnp.concatenate(blocks, axis=0).astype(MXU_DTYPE)

    def new_rows(rows):
        a = jnp.concatenate([rows, jnp.zeros((2 * t_new - rows.shape[0], 2 * KV_W), F32)], axis=0)
        return a[:, :KV_W].astype(MXU_DTYPE), a[:, KV_W:].astype(MXU_DTYPE)

    def joint_softmax(s1, ok1, s2, ok2):
        s1 = jnp.where(ok1, s1, NEG)
        s2 = jnp.where(ok2, s2, NEG)
        m = jnp.maximum(jnp.max(s1, axis=1, keepdims=True), jnp.max(s2, axis=1, keepdims=True))
        e1 = jnp.where(ok1, jnp.exp(s1 - m), 0.0)
        e2 = jnp.where(ok2, jnp.exp(s2 - m), 0.0)
        l = jnp.maximum(jnp.sum(e1, axis=1, keepdims=True) + jnp.sum(e2, axis=1, keepdims=True), 1e-30)
        return e1 / l, e2 / l

    kwn, vwn = new_rows(kvw_ref[0])
    p1, p2 = joint_softmax(_dot(qbd, cw_ref[0, 0:KV_W, :].astype(MXU_DTYPE)), wmask_ref[...] > 0.5,
                           _dot_nt(qbd, kwn), wmaskn_ref[...] > 0.5)
    o_w = (_dot_nt(p1.astype(MXU_DTYPE), cw_ref[0, KV_W:2 * KV_W, :].astype(MXU_DTYPE))
           + _dot(p2.astype(MXU_DTYPE), vwn))

    wb = cw_ref.shape[2]
    shifted = pltpu.roll(cw_ref[0], wb - t_new, 1)
    new_t = jnp.concatenate([jnp.zeros((LANES - t_new, 2 * KV_W), F32), kvw_ref[0]], axis=0).T
    wlane = lax.broadcasted_iota(jnp.int32, (2 * KV_W, LANES), 1)
    wout_ref[0, :, 0:wb - LANES] = shifted[:, 0:wb - LANES]
    wout_ref[0, :, wb - LANES:wb] = jnp.where(wlane >= LANES - t_new, new_t, shifted[:, wb - LANES:wb])

    for cpy in copies(b, slot):
        cpy.wait()

    nhalf = npages // HALF_PAGES
    summ_scr[...] = jnp.zeros(summ_scr.shape, F32)

    def cmp_body(it, _):
        xs = []
        for k in range(CMP_PAGES_PER_STEP):
            pg = it * CMP_PAGES_PER_STEP + k
            x = jnp.concatenate([cbuf[slot, h2 * HALF_PAGES + pg] * cpt_ref[...] for h2 in range(nhalf)], axis=0)
            xs.append(x.astype(MXU_DTYPE))
        summ_scr[...] += _dot(jnp.concatenate(xs, axis=1), gsum_ref[it])
        return 0

    lax.fori_loop(0, HALF_PAGES // CMP_PAGES_PER_STEP, cmp_body, 0)
    summ = jnp.concatenate([summ_scr[h2 * 2 * KV_W:(h2 + 1) * 2 * KV_W] for h2 in range(nhalf)], axis=1)
    ckv = _dot(wct_ref[...], summ.astype(MXU_DTYPE))
    kc = ckv[0:KV_W]
    ckt = (kc * cc_ref[...] + pltpu.roll(kc, ROT_HALF, 0) * cs1_ref[...]
           + pltpu.roll(kc, KV_W - ROT_HALF, 0) * cs2_ref[...]).astype(MXU_DTYPE)
    cvt = ckv[KV_W:2 * KV_W].astype(MXU_DTYPE)

    ckn, cvn = new_rows(ckv0_ref[0])
    p1, p2 = joint_softmax(_dot(qbd, ckt), cmask_ref[...] > 0.5, _dot_nt(qbd, ckn), cmaskn_ref[...] > 0.5)
    o_c = _dot_nt(p1.astype(MXU_DTYPE), cvt) + _dot(p2.astype(MXU_DTYPE), cvn)
    pc = p1[0:ngt]
    for h in range(1, HPG):
        pc = pc + p1[h * ngt:(h + 1) * ngt]
    ntile = pc.shape[1] // LANES
    pair = [pc[:, t * LANES:(t + 1) * LANES] + pltpu.roll(pc[:, t * LANES:(t + 1) * LANES], LANES - HALF_PAGES, 1)
            for t in range(ntile)]
    pair = jnp.concatenate([pr[:, HALF_PAGES:2 * HALF_PAGES] for pr in pair]
                           + [pr[:, 3 * HALF_PAGES:LANES] for pr in pair], axis=1)

    cand = cand_ref[...] > 0.5
    sc = jnp.where(cand, pair, -1.0)
    jl = jidx_ref[...]
    rank = jnp.zeros(sc.shape, F32)
    for i in range(sc.shape[1]):
        col = sc[:, i:i + 1]
        beats = (col > sc) | ((col == sc) & (jl > int(jidx_np[i])))
        rank = rank + jnp.where(beats, 1.0, 0.0)
    sel_eo = jnp.where((cand & (rank < n_top)) | (forced_ref[...] > 0.5), 1.0, 0.0).astype(MXU_DTYPE)

    def score_body(ch, mel):
        keep = _dot(sel_eo, eeo_ref[ch]) > 0.5
        bias = jnp.where(keep, 0.0, NEG)
        bias = jnp.concatenate([bias] * HPG, axis=0)
        for pp in range(0, PAGE_CHUNK, 2):
            pg = ch * PAGE_CHUNK + pp
            kt = jnp.concatenate([sbuf[slot, pg, 0:KV_W, :], sbuf[slot, pg + 1, 0:KV_W, :]], axis=1)
            sv = _dot(qbd, kt.astype(MXU_DTYPE)) + bias[:, pp * LANES:(pp + 2) * LANES]
            s_scr[pg] = sv[:, 0:LANES]
            s_scr[pg + 1] = sv[:, LANES:2 * LANES]
            mel = jnp.maximum(mel, jnp.maximum(sv[:, 0:LANES], sv[:, LANES:2 * LANES]))
        return mel

    mel = lax.fori_loop(0, npages // PAGE_CHUNK, score_body, jnp.full((nrow, LANES), NEG, F32))
    ksn, vsn = new_rows(kvs_ref[0])
    nok = nmask_ref[...] > 0.5
    s_new = jnp.where(nok, _dot_nt(qbd, ksn), NEG)
    m = jnp.maximum(jnp.max(mel, axis=1, keepdims=True), jnp.max(s_new, axis=1, keepdims=True))
    mb = jnp.broadcast_to(m, (nrow, LANES))

    def value_body(ch, carry):
        lel, acc = carry
        for pp in range(0, PAGE_CHUNK, 2):
            pg = ch * PAGE_CHUNK + pp
            pv0 = jnp.exp(s_scr[pg] - mb)
            pv1 = jnp.exp(s_scr[pg + 1] - mb)
            lel = lel + (pv0 + pv1)
            vt = jnp.concatenate([sbuf[slot, pg, KV_W:2 * KV_W, :], sbuf[slot, pg + 1, KV_W:2 * KV_W, :]], axis=1)
            acc = acc + _dot_nt(jnp.concatenate([pv0, pv1], axis=1).astype(MXU_DTYPE), vt.astype(MXU_DTYPE))
        return lel, acc

    e_new = jnp.where(nok, jnp.exp(s_new - m), 0.0)
    lel, acc = lax.fori_loop(0, npages // PAGE_CHUNK, value_body,
                             (jnp.zeros((nrow, LANES), F32), _dot(e_new.astype(MXU_DTYPE), vsn)))
    l = jnp.sum(lel, axis=1, keepdims=True) + jnp.sum(e_new, axis=1, keepdims=True)
    o_s = acc / jnp.maximum(l, 1e-30)

    gs = jax.nn.sigmoid(gt_ref[0])
    outs = [None] * N_HEADS
    for h in range(HPG):
        for g in range(KV_HEADS):
            hh = g * HPG + h
            rs = slice((h * KV_HEADS + g) * t_new, (h * KV_HEADS + g + 1) * t_new)
            cs = slice(g * HEAD_DIM, (g + 1) * HEAD_DIM)
            col = hh * N_BRANCH
            outs[hh] = (gs[:, col:col + 1] * o_c[rs, cs] + gs[:, col + 1:col + 2] * o_s[rs, cs]
                        + gs[:, col + 2:col + 3] * o_w[rs, cs])
    o_ref[0] = jnp.concatenate(outs, axis=1)


def _dims_major(a):
    n, toks = a.shape[:2]
    return jnp.transpose(a, (0, 2, 3, 4, 1)).reshape(n, 2 * KV_W, toks)


def _attn_sample(q, gates, kvs, kvw, ckv0, cache_cmp, cache_sel, cache_win, page_table, wts, t_new):
    nseq, npages = page_table.shape
    past = npages * PAGE_SIZE
    wb = cache_win.shape[1]
    nrow = HPG * KV_HEADS * t_new
    ngt = KV_HEADS * t_new
    nlane = npages * CMP_LANE_BLKS
    assert t_new <= L_CMP and past % L_SEL == 0 and wb == WINDOW and wb % LANES == 0
    assert npages % HALF_PAGES == 0 and npages % PAGE_CHUNK == 0 and 2 * npages == LANES
    tb = _sample_tables(past, t_new, wb)
    cc, cs1, cs2 = _rope_tables_t((tb["blk"] + 1) * L_CMP - 1)
    cpt = jnp.tile(wts["cp"].T, (1, CMP_LANE_BLKS))
    seq3 = lambda r, w: pl.BlockSpec((1, r, w), lambda b, pt: (b, 0, 0))
    const = lambda *s: pl.BlockSpec(s, lambda b, pt: (0,) * len(s), pipeline_mode=pl.Buffered(1))
    hbm = pl.BlockSpec(memory_space=pl.ANY)
    in_specs = [seq3(t_new, ATT_W), seq3(t_new, LANES), seq3(t_new, 2 * KV_W), seq3(t_new, 2 * KV_W),
                seq3(1, 2 * KV_W), seq3(2 * KV_W, wb), hbm, hbm,
                const(2 * KV_W, PAGE_SIZE), const(2 * KV_W, 2 * KV_W), const(KV_W, nlane), const(KV_W, nlane),
                const(KV_W, nlane), const(nrow, nlane), const(nrow, 2 * t_new), const(ngt, 2 * npages),
                const(ngt, 2 * npages), const(nrow, 2 * t_new), const(nrow, wb), const(nrow, 2 * t_new),
                const(1, 2 * npages), const(npages // PAGE_CHUNK, 2 * npages, PAGE_CHUNK * PAGE_SIZE),
                const(HALF_PAGES // CMP_PAGES_PER_STEP, CMP_PAGES_PER_STEP * PAGE_SIZE, LANES)]
    grid_spec = pltpu.PrefetchScalarGridSpec(
        num_scalar_prefetch=1, grid=(nseq,), in_specs=in_specs,
        out_specs=(seq3(t_new, ATT_W), seq3(2 * KV_W, wb)),
        scratch_shapes=[pltpu.VMEM((2, npages, 2 * KV_W, PAGE_SIZE), F32),
                        pltpu.VMEM((2, npages, 2 * KV_W, PAGE_SIZE), F32),
                        pltpu.SemaphoreType.DMA((2, 2)),
                        pltpu.VMEM((nlane // LANES * 2 * KV_W, LANES), F32),
                        pltpu.VMEM((npages, nrow, PAGE_SIZE), F32)])
    return pl.pallas_call(
        functools.partial(_attn_sample_kernel, jidx_np=tb["jidx_np"], n_top=N_TOP),
        grid_spec=grid_spec,
        out_shape=(jax.ShapeDtypeStruct((nseq, t_new, ATT_W), F32),
                   jax.ShapeDtypeStruct((nseq, 2 * KV_W, wb), F32)),
        compiler_params=pltpu.CompilerParams(dimension_semantics=("arbitrary",),
                                             vmem_limit_bytes=SAMPLE_VMEM_LIMIT),
    )(page_table, q.reshape(nseq, t_new, ATT_W), gates.reshape(nseq, t_new, LANES),
      kvs.reshape(nseq, t_new, 2 * KV_W), kvw.reshape(nseq, t_new, 2 * KV_W), ckv0.reshape(nseq, 1, 2 * KV_W),
      _dims_major(cache_win), _dims_major(cache_cmp), _dims_major(cache_sel), cpt, wts["wc"].T, cc, cs1, cs2,
      tb["cmask"], tb["cmask_new"], tb["cand"], tb["forced"], tb["nmask"], tb["wmask"], tb["wmask_new"],
      tb["jidx"], tb["eeo"], tb["gsum"])


def _sample_group(x_sample, cache_cmp, cache_sel, cache_win, state_pool, page_table, g_pre_mix, g_post_mix,
                  g_pre_ffn, g_post_ffn, wts):
    nseq, t_new, _ = x_sample.shape
    past = page_table.shape[1] * PAGE_SIZE
    wb = cache_win.shape[1]
    x2 = x_sample.reshape(nseq * t_new, D_MODEL)
    hist16 = jnp.pad(state_pool, ((0, 0), (HIST_ROWS - POOL_HIST, 0), (0, 0)))
    q, kvc, kvs, kvw, gates, pool_y, znew, ckv0 = _inproj_sample(x2, g_pre_mix, wts, hist16, nseq, t_new, past)
    o_att, win_t = _attn_sample(q, gates, kvs, kvw, ckv0, cache_cmp, cache_sel, cache_win, page_table, wts, t_new)
    y = _post(x2, o_att.reshape(nseq * t_new, ATT_W), pool_y, g_post_mix, g_pre_ffn, g_post_ffn, wts)
    kv6 = lambda a: a.reshape(1, nseq, t_new, 2, KV_HEADS, HEAD_DIM)
    win_new = jnp.transpose(win_t.reshape(1, nseq, 2, KV_HEADS, HEAD_DIM, wb), (0, 1, 5, 2, 3, 4))
    return (y.reshape(nseq, t_new, D_MODEL), kv6(kvc), kv6(kvs), win_new, znew[None, :, 1:, :])


def kernel(x_prompt, x_sample, cache_kv_cmp, cache_kv_sel, cache_kv_win, state_pool, page_table, g_pre_mix,
           g_post_mix, g_pre_ffn, g_post_ffn, w_in, cmp_pos, w_cmp, w_pool, pool_scale, w_out, w_gate, w_up, w_down):
    wts = _prep_weights(w_in[0], cmp_pos[0], w_cmp[0], w_pool[0], pool_scale[0], w_out[0], w_gate[0], w_up[0],
                        w_down[0])
    yp, cmp_p, sel_p, win_p, pool_p = _prompt_group(x_prompt, g_pre_mix[0], g_post_mix[0], g_pre_ffn[0],
                                                    g_post_ffn[0], wts)
    ys, cmp_s, sel_s, win_s, pool_s = _sample_group(x_sample, cache_kv_cmp[0], cache_kv_sel[0], cache_kv_win[0],
                                                    state_pool[0], page_table, g_pre_mix[0], g_post_mix[0],
                                                    g_pre_ffn[0], g_post_ffn[0], wts)
    return (yp, ys, cmp_p, sel_p, win_p, pool_p, cmp_s, sel_s, win_s, pool_s)
```

```python
import functools

import jax
import jax.numpy as jnp
import numpy as np
from jax import lax
from jax.experimental import pallas as pl
from jax.experimental.pallas import tpu as pltpu

F32 = jnp.float32
MXU_DTYPE = jnp.bfloat16

D_MODEL = 1024
PAGE_SIZE = 128
N_HEADS = 8
KV_HEADS = 2
HPG = N_HEADS // KV_HEADS
HEAD_DIM = 64
ATT_W = N_HEADS * HEAD_DIM
KV_W = KV_HEADS * HEAD_DIM
N_BRANCH = 3
L_CMP = 32
L_SEL = 64
N_SEL = 16
N_FORCED = 3
N_TOP = N_SEL - N_FORCED
WINDOW = 512
SCALE = HEAD_DIM ** -0.5
ROT_DIM = HEAD_DIM // 4
ROT_HALF = ROT_DIM // 2
ROPE_THETA = 500000.0
POOL_WINDOWS = (2, 4, 8, 16)
POOL_W = D_MODEL - ATT_W
POOL_GC = POOL_W // len(POOL_WINDOWS)
POOL_HIST = max(POOL_WINDOWS) - 1
HIST_ROWS = POOL_HIST + 1
N_GATE = N_HEADS * N_BRANCH
D_FF = -(-8 * D_MODEL // (3 * 256)) * 256
EPS = 1e-6
NEG = -1e30

LANES = 128
C_Q = 0
C_KVC = C_Q + ATT_W
C_KVS = C_KVC + 2 * KV_W
C_KVW = C_KVS + 2 * KV_W
C_POOL = C_KVW + 2 * KV_W
C_GATE = C_POOL + POOL_W
IN_W_PAD = C_GATE + LANES

TM_IN = 1024
TM_POST = 1024
Q_BLK = 256
K_CHUNK = 512
FF_CHUNKS = 11
VMEM_LIMIT = 56 * 1024 * 1024
SAMPLE_VMEM_LIMIT = 60 * 1024 * 1024


def _const_spec(shape):
    nd = len(shape)
    return pl.BlockSpec(shape, lambda *_: (0,) * nd, pipeline_mode=pl.Buffered(1))


def _rms(x, g):
    return x * lax.rsqrt(jnp.mean(x * x, axis=-1, keepdims=True) + EPS) * g


def _rope128(x, c, s1, s2):
    return x * c + pltpu.roll(x, ROT_HALF, 1) * s1 + pltpu.roll(x, LANES - ROT_HALF, 1) * s2


def _dot(a, b):
    return jnp.dot(a, b, preferred_element_type=F32)


def _dot_nt(a, b):
    return lax.dot_general(a, b, (((1,), (1,)), ((), ())), preferred_element_type=F32)


def _dot_tn(a, b):
    return lax.dot_general(a, b, (((0,), (0,)), ((), ())), preferred_element_type=F32)


def _rope_tables(pos):
    pos = jnp.asarray(np.asarray(pos), jnp.int32)
    inv = ROPE_THETA ** (-jnp.arange(ROT_HALF, dtype=F32) * 2.0 / ROT_DIM)
    ang = pos.astype(F32)[:, None] * inv
    cos, sin = jnp.cos(ang), jnp.sin(ang)
    n = pos.shape[0]
    one = jnp.ones((n, HEAD_DIM - ROT_DIM), F32)
    zero = jnp.zeros((n, HEAD_DIM - ROT_DIM), F32)
    zh = jnp.zeros((n, ROT_HALF), F32)
    c = jnp.concatenate([cos, cos, one], axis=1)
    s1 = jnp.concatenate([zh, sin, zero], axis=1)
    s2 = jnp.concatenate([-sin, zh, zero], axis=1)
    tile = lambda t: jnp.concatenate([t, t], axis=1)
    return tile(c), tile(s1), tile(s2)


def _prep_weights(w_in, cmp_pos, w_cmp, w_pool, pool_scale, w_out, w_gate, w_up, w_down):
    w_in_r = jnp.concatenate([
        w_in[:, :ATT_W + 6 * KV_W],
        w_in[:, ATT_W + 6 * KV_W + N_GATE:],
        w_in[:, ATT_W + 6 * KV_W:ATT_W + 6 * KV_W + N_GATE],
        jnp.zeros((D_MODEL, LANES - N_GATE), w_in.dtype)], axis=1).astype(MXU_DTYPE)
    cp = jnp.concatenate([cmp_pos[:, 0], cmp_pos[:, 0], cmp_pos[:, 1], cmp_pos[:, 1]], axis=1)
    wc = jnp.zeros((2 * KV_W, 2 * KV_W), F32)
    for i, c in enumerate((0, 0, 1, 1)):
        wc = wc.at[i * HEAD_DIM:(i + 1) * HEAD_DIM, i * HEAD_DIM:(i + 1) * HEAD_DIM].set(w_cmp[c])
    fc = D_FF // FF_CHUNKS
    return dict(
        w_in=w_in_r, cp=cp, wc=wc.astype(MXU_DTYPE), w_pool=w_pool.astype(MXU_DTYPE),
        pool_scale=pool_scale.reshape(1, POOL_W), w_out=w_out.astype(MXU_DTYPE),
        w_gate=w_gate.reshape(D_MODEL, FF_CHUNKS, fc).transpose(1, 0, 2).astype(MXU_DTYPE),
        w_up=w_up.reshape(D_MODEL, FF_CHUNKS, fc).transpose(1, 0, 2).astype(MXU_DTYPE),
        w_down=w_down.reshape(FF_CHUNKS, fc, D_MODEL).astype(MXU_DTYPE))


def _project_rows(x, g, w_ref, rc, rs1, rs2):
    h = _rms(x, g).astype(MXU_DTYPE)
    z = _dot(h, w_ref[...])
    q = jnp.concatenate(
        [_rope128(z[:, C_Q + i * LANES:C_Q + (i + 1) * LANES], rc, rs1, rs2) for i in range(ATT_W // LANES)],
        axis=1) * SCALE
    kvc = z[:, C_KVC:C_KVS]
    kvs = jnp.concatenate([_rope128(z[:, C_KVS:C_KVS + KV_W], rc, rs1, rs2), z[:, C_KVS + KV_W:C_KVW]], axis=1)
    kvw = jnp.concatenate([_rope128(z[:, C_KVW:C_KVW + KV_W], rc, rs1, rs2), z[:, C_KVW + KV_W:C_POOL]], axis=1)
    u = z[:, C_POOL:C_GATE]
    gates = z[:, C_GATE:IN_W_PAD]
    return q, kvc, kvs, kvw, u, gates


def _compress_rows(kvc, cp, wc_ref, cc, cs1, cs2):
    n = kvc.shape[0] // L_CMP
    summ = jnp.sum(kvc.reshape(n, L_CMP, 2 * KV_W) * cp[None], axis=1)
    ckv = _dot(summ.astype(MXU_DTYPE), wc_ref[...])
    return jnp.concatenate([_rope128(ckv[:, :KV_W], cc, cs1, cs2), ckv[:, KV_W:]], axis=1)


def _pool_windows(zf, rows, cnt):
    outs = []
    for gi, w in enumerate(POOL_WINDOWS):
        a = zf[:, gi * POOL_GC:(gi + 1) * POOL_GC]
        s, sh = a, 1
        while sh < w:
            s = s + pltpu.roll(s, sh, 0)
            sh *= 2
        outs.append(rows(s) / cnt[gi] - rows(a))
    return outs


def _pool_project(d_list, wp_ref, ps):
    y = jnp.concatenate([_dot(d.astype(MXU_DTYPE), wp_ref[gi]) for gi, d in enumerate(d_list)], axis=1)
    return y * ps


def _inproj_prompt_kernel(x_ref, g_ref, w_ref, rc_ref, rs1_ref, rs2_ref, cp_ref, wc_ref, cc_ref, cs1_ref, cs2_ref,
                          wp_ref, ps_ref,
                          q_ref, kvc_ref, kvs_ref, kvw_ref, kvsb_ref, kvwb_ref, gates_ref, pool_ref, ulast_ref,
                          ckv_ref, z_scr):
    j = pl.program_id(1)
    tm = x_ref.shape[0]
    nblk = tm // L_CMP
    r0 = pl.multiple_of(j * tm, tm)
    rows = pl.ds(r0, tm)
    q, kvc, kvs, kvw, u, gates = _project_rows(x_ref[...], g_ref[...], w_ref, rc_ref[rows, :], rs1_ref[rows, :],
                                               rs2_ref[rows, :])
    q_ref[...] = q.astype(q_ref.dtype)
    kvc_ref[0] = kvc.T
    kvs_ref[0] = kvs.T
    kvw_ref[0] = kvw.T
    kvsb_ref[...] = kvs.astype(kvsb_ref.dtype)
    kvwb_ref[...] = kvw.astype(kvwb_ref.dtype)
    gates_ref[...] = gates

    crow = pl.ds(pl.multiple_of(j * nblk, nblk), nblk)
    ckv_ref[...] = _compress_rows(kvc, cp_ref[...], wc_ref, cc_ref[crow, :], cs1_ref[crow, :], cs2_ref[crow, :])

    @pl.when(j == 0)
    def _():
        z_scr[0:HIST_ROWS, :] = jnp.zeros((HIST_ROWS, POOL_W), F32)

    @pl.when(j > 0)
    def _():
        z_scr[0:HIST_ROWS, :] = z_scr[tm:tm + HIST_ROWS, :]

    z_scr[HIST_ROWS:, :] = u
    pos = (r0 + lax.broadcasted_iota(jnp.int32, (tm, 1), 0)).astype(F32)
    cnt = [jnp.minimum(pos + 1.0, float(w)) for w in POOL_WINDOWS]
    d_list = _pool_windows(z_scr[...], lambda a: a[HIST_ROWS:, :], cnt)
    pool_ref[...] = _pool_project(d_list, wp_ref, ps_ref[...]).astype(pool_ref.dtype)
    ulast_ref[0] = u[tm - HIST_ROWS:, :]


def _inproj_prompt(x2, g_pre, wts, seq, batch):
    n = x2.shape[0]
    tm = min(TM_IN, seq)
    nj = seq // tm
    nblk = tm // L_CMP
    rc, rs1, rs2 = _rope_tables(np.arange(seq))
    cc, cs1, cs2 = _rope_tables((np.arange(seq // L_CMP) + 1) * L_CMP - 1)
    row = lambda w: pl.BlockSpec((tm, w), lambda b, j: (b * nj + j, 0))
    out_shape = (
        jax.ShapeDtypeStruct((n, ATT_W), MXU_DTYPE),
        jax.ShapeDtypeStruct((batch, 2 * KV_W, seq), F32),
        jax.ShapeDtypeStruct((batch, 2 * KV_W, seq), F32),
        jax.ShapeDtypeStruct((batch, 2 * KV_W, seq), F32),
        jax.ShapeDtypeStruct((n, 2 * KV_W), MXU_DTYPE),
        jax.ShapeDtypeStruct((n, 2 * KV_W), MXU_DTYPE),
        jax.ShapeDtypeStruct((n, LANES), F32),
        jax.ShapeDtypeStruct((n, POOL_W), MXU_DTYPE),
        jax.ShapeDtypeStruct((batch, HIST_ROWS, POOL_W), F32),
        jax.ShapeDtypeStruct((n // L_CMP, 2 * KV_W), F32),
    )
    dims_major = pl.BlockSpec((1, 2 * KV_W, tm), lambda b, j: (b, 0, j))
    out_specs = (row(ATT_W), dims_major, dims_major, dims_major, row(2 * KV_W), row(2 * KV_W), row(LANES),
                 row(POOL_W), pl.BlockSpec((1, HIST_ROWS, POOL_W), lambda b, j: (b, 0, 0)),
                 pl.BlockSpec((nblk, 2 * KV_W), lambda b, j: (b * nj + j, 0)))
    in_specs = [row(D_MODEL), _const_spec((1, D_MODEL)), _const_spec((D_MODEL, IN_W_PAD)),
                _const_spec((seq, LANES)), _const_spec((seq, LANES)), _const_spec((seq, LANES)),
                _const_spec((L_CMP, 2 * KV_W)), _const_spec((2 * KV_W, 2 * KV_W)),
                _const_spec((seq // L_CMP, LANES)), _const_spec((seq // L_CMP, LANES)),
                _const_spec((seq // L_CMP, LANES)),
                _const_spec((len(POOL_WINDOWS), POOL_GC, POOL_GC)), _const_spec((1, POOL_W))]
    return pl.pallas_call(
        _inproj_prompt_kernel,
        grid=(batch, nj),
        in_specs=in_specs, out_specs=out_specs, out_shape=out_shape,
        scratch_shapes=[pltpu.VMEM((tm + HIST_ROWS, POOL_W), F32)],
        compiler_params=pltpu.CompilerParams(dimension_semantics=("arbitrary", "arbitrary"),
                                             vmem_limit_bytes=VMEM_LIMIT),
    )(x2, g_pre.reshape(1, D_MODEL), wts["w_in"], rc, rs1, rs2, wts["cp"], wts["wc"], cc, cs1, cs2,
      wts["w_pool"], wts["pool_scale"])


def _post_kernel(x_ref, o_ref, p_ref, wo_ref, gpm_ref, gpf_ref, gqf_ref, wg_ref, wu_ref, wd_ref, y_ref):
    mixed = _dot(jnp.concatenate([o_ref[...].astype(MXU_DTYPE), p_ref[...]], axis=1), wo_ref[...])
    x1 = x_ref[...] + _rms(mixed, gpm_ref[...])
    h2 = _rms(x1, gpf_ref[...]).astype(MXU_DTYPE)
    acc = None
    for c in range(FF_CHUNKS):
        f = (jax.nn.silu(_dot(h2, wg_ref[c])) * _dot(h2, wu_ref[c])).astype(MXU_DTYPE)
        part = _dot(f, wd_ref[c])
        acc = part if acc is None else acc + part
    y_ref[...] = x1 + _rms(acc, gqf_ref[...])


def _post(x2, o_att, pool_y, g_post_mix, g_pre_ffn, g_post_ffn, wts):
    n = x2.shape[0]
    tm = min(TM_POST, n)
    fc = D_FF // FF_CHUNKS
    row = lambda w: pl.BlockSpec((tm, w), lambda i: (i, 0))
    return pl.pallas_call(
        _post_kernel,
        grid=(n // tm,),
        in_specs=[row(D_MODEL), row(ATT_W), row(POOL_W), _const_spec((D_MODEL, D_MODEL)),
                  _const_spec((1, D_MODEL)), _const_spec((1, D_MODEL)), _const_spec((1, D_MODEL)),
                  _const_spec((FF_CHUNKS, D_MODEL, fc)), _const_spec((FF_CHUNKS, D_MODEL, fc)),
                  _const_spec((FF_CHUNKS, fc, D_MODEL))],
        out_specs=row(D_MODEL),
        out_shape=jax.ShapeDtypeStruct((n, D_MODEL), F32),
        compiler_params=pltpu.CompilerParams(dimension_semantics=("arbitrary",), vmem_limit_bytes=VMEM_LIMIT),
    )(x2, o_att, pool_y, wts["w_out"], g_post_mix.reshape(1, D_MODEL), g_pre_ffn.reshape(1, D_MODEL),
      g_post_ffn.reshape(1, D_MODEL), wts["w_gate"], wts["w_up"], wts["w_down"])


def _unselected_blocks_t(p_slc_t, c_row, n_top):
    nb = p_slc_t.shape[0]
    jj = lax.broadcasted_iota(jnp.int32, p_slc_t.shape, 0)
    cand = (jj >= 1) & (jj <= c_row - 2)
    sc = jnp.where(cand, p_slc_t, -1.0)
    rank = jnp.zeros(sc.shape, F32)
    for i in range(nb):
        row = sc[i:i + 1, :]
        beats = (row > sc) | ((row == sc) & (jj > i))
        rank = rank + jnp.where(beats, 1.0, 0.0)
    sel = (cand & (rank < n_top)) | (jj == 0) | (jj == c_row) | (jj == c_row - 1)
    return jnp.where(sel, 0.0, 1.0)


def _lane_tiles(a, n):
    return jnp.concatenate([a] * n, axis=1)


def _attn_prompt_kernel(q_ref, gt_ref, ckv_ref, ks_ref, kw_ref, eneg_ref, o_ref,
                        qaug_scr, oc_scr, s_scr, mel_scr, mb_scr, lel_scr, acc_scr):
    qb = pl.program_id(1)
    qblk = q_ref.shape[0]
    rows = HPG * qblk
    ncb = ckv_ref.shape[1]
    half = ncb // 2
    wspan = WINDOW + qblk
    start = qb * qblk
    tpos_c = start + lax.broadcasted_iota(jnp.int32, (qblk, 1), 0)
    tpos_r = start + lax.broadcasted_iota(jnp.int32, (1, qblk), 1)
    gs = jax.nn.sigmoid(gt_ref[...])
    ckv = ckv_ref[0]
    ck = ckv[:, :KV_W].astype(MXU_DTYPE)
    cv = ckv[:, KV_W:].astype(MXU_DTYPE)
    crow = lax.broadcasted_iota(jnp.int32, (ncb, rows), 0)
    cblk = 2 * (crow % half) + crow // half
    cmask = ((cblk + 1) * L_CMP - 1) <= _lane_tiles(tpos_r, HPG)

    ws = pl.multiple_of(jnp.maximum(start - WINDOW, 0), qblk)
    dwin = tpos_c - (ws + lax.broadcasted_iota(jnp.int32, (1, wspan), 1))
    bias_w = jnp.where((dwin >= 0) & (dwin < WINDOW), 0.0, NEG)
    n_chunks = start // K_CHUNK + 1
    lane = lax.broadcasted_iota(jnp.int32, (qblk, LANES), 1)

    outs = [None] * N_HEADS
    for g in range(KV_HEADS):
        pieces = []
        for h in range(HPG):
            hh = g * HPG + h
            tile = q_ref[:, (hh // 2) * LANES:(hh // 2 + 1) * LANES].astype(F32)
            if hh % 2 != g:
                tile = pltpu.roll(tile, HEAD_DIM, 1)
            pieces.append(jnp.where((lane >= g * HEAD_DIM) & (lane < (g + 1) * HEAD_DIM), tile, 0.0))
        qg = jnp.concatenate(pieces, axis=0).astype(MXU_DTYPE)

        s = jnp.where(cmask, _dot_nt(ck, qg), NEG)
        e = jnp.where(cmask, jnp.exp(s - jnp.max(s, axis=0, keepdims=True)), 0.0)
        p = e / jnp.maximum(jnp.sum(e, axis=0, keepdims=True), 1e-30)
        o_c = _dot_tn(p.astype(MXU_DTYPE), cv)
        pc = p[:, 0:qblk]
        for h in range(1, HPG):
            pc = pc + p[:, h * qblk:(h + 1) * qblk]
        unsel = _unselected_blocks_t(pc[0:half] + pc[half:ncb], tpos_r // L_SEL, N_TOP)
        unsel = jnp.concatenate([unsel, jnp.zeros((LANES - half, qblk), F32)], axis=0)
        unsel = unsel.T.astype(MXU_DTYPE)
        gr = slice(g * rows, (g + 1) * rows)
        qaug_scr[gr, :] = jnp.concatenate([qg, jnp.concatenate([unsel] * HPG, axis=0)], axis=1)
        oc_scr[gr, :] = o_c

    mel_scr[...] = jnp.full((KV_HEADS * rows, LANES), NEG, F32)

    def score_chunk(ci, causal):
        k0 = pl.multiple_of(ci * K_CHUNK, K_CHUNK)
        kaug = jnp.concatenate([ks_ref[0, pl.ds(k0, K_CHUNK), 0:KV_W], eneg_ref[pl.ds(k0, K_CHUNK), :]], axis=1)
        sv = _dot_nt(qaug_scr[...], kaug)
        if causal:
            kpos = k0 + lax.broadcasted_iota(jnp.int32, (1, K_CHUNK), 1)
            bias = jnp.where(kpos <= tpos_c, 0.0, NEG)
            sv = jnp.concatenate([sv[r * qblk:(r + 1) * qblk] + bias for r in range(KV_HEADS * HPG)], axis=0)
        s_scr[ci] = sv
        mh = sv[:, 0:LANES]
        for j in range(1, K_CHUNK // LANES):
            mh = jnp.maximum(mh, sv[:, j * LANES:(j + 1) * LANES])
        mel_scr[...] = jnp.maximum(mel_scr[...], mh)

    def score_body(ci, _):
        score_chunk(ci, False)
        return 0

    lax.fori_loop(0, n_chunks - 1, score_body, 0)
    score_chunk(n_chunks - 1, True)
    mb_scr[...] = jnp.broadcast_to(jnp.max(mel_scr[...], axis=1, keepdims=True), (KV_HEADS * rows, LANES))
    lel_scr[...] = jnp.zeros((KV_HEADS * rows, LANES), F32)
    acc_scr[...] = jnp.zeros((KV_HEADS * rows, LANES), F32)

    def value_body(ci, _):
        k0 = pl.multiple_of(ci * K_CHUNK, K_CHUNK)
        pv = jnp.exp(s_scr[ci] - _lane_tiles(mb_scr[...], K_CHUNK // LANES))
        lsum = pv[:, 0:LANES]
        for j in range(1, K_CHUNK // LANES):
            lsum = lsum + pv[:, j * LANES:(j + 1) * LANES]
        lel_scr[...] += lsum
        acc_scr[...] += _dot(pv.astype(MXU_DTYPE), ks_ref[0, pl.ds(k0, K_CHUNK), KV_W:2 * KV_W])
        return 0

    lax.fori_loop(0, n_chunks, value_body, 0)
    acc_scr[...] = acc_scr[...] / jnp.maximum(jnp.sum(lel_scr[...], axis=1, keepdims=True), 1e-30)

    for g in range(KV_HEADS):
        gr = slice(g * rows, (g + 1) * rows)
        o_c = oc_scr[gr, :]
        o_s = acc_scr[gr, :]
        qg = qaug_scr[gr, 0:KV_W]
        sw = _dot_nt(qg, kw_ref[0, pl.ds(ws, wspan), 0:KV_W])
        sw = jnp.concatenate([sw[h * qblk:(h + 1) * qblk] + bias_w for h in range(HPG)], axis=0)
        ew = jnp.exp(sw - jnp.max(sw, axis=1, keepdims=True))
        o_w = (_dot(ew.astype(MXU_DTYPE), kw_ref[0, pl.ds(ws, wspan), KV_W:2 * KV_W])
               / jnp.maximum(jnp.sum(ew, axis=1, keepdims=True), 1e-30))

        for h in range(HPG):
            hh = g * HPG + h
            col = hh * N_BRANCH
            rs = slice(h * qblk, (h + 1) * qblk)
            cs = slice(g * HEAD_DIM, (g + 1) * HEAD_DIM)
            outs[hh] = (gs[:, col:col + 1] * o_c[rs, cs] + gs[:, col + 1:col + 2] * o_s[rs, cs]
                        + gs[:, col + 2:col + 3] * o_w[rs, cs])
    o_ref[...] = jnp.concatenate(outs, axis=1).astype(o_ref.dtype)


def _attn_prompt(q, gates, ckv_perm, kvsb, kvwb, seq, batch):
    n = q.shape[0]
    qblk = min(Q_BLK, seq)
    nqb = seq // qblk
    nb = seq // L_SEL
    nchunk = seq // K_CHUNK
    rows = HPG * qblk
    assert seq >= WINDOW + qblk and seq % K_CHUNK == 0 and K_CHUNK % qblk == 0
    assert nb <= LANES
    eneg = jnp.asarray(NEG * (np.arange(seq)[:, None] // L_SEL == np.arange(LANES)[None, :]), MXU_DTYPE)
    row = lambda w: pl.BlockSpec((qblk, w), lambda b, i: (b * nqb + i, 0))
    per_b = lambda r, w: pl.BlockSpec((1, r, w), lambda b, i: (b, 0, 0))
    return pl.pallas_call(
        _attn_prompt_kernel,
        grid=(batch, nqb),
        in_specs=[row(ATT_W), row(LANES), per_b(seq // L_CMP, 2 * KV_W), per_b(seq, 2 * KV_W),
                  per_b(seq, 2 * KV_W), _const_spec((seq, LANES))],
        out_specs=row(ATT_W),
        out_shape=jax.ShapeDtypeStruct((n, ATT_W), MXU_DTYPE),
        scratch_shapes=[pltpu.VMEM((KV_HEADS * rows, 2 * LANES), MXU_DTYPE),
                        pltpu.VMEM((KV_HEADS * rows, LANES), F32),
                        pltpu.VMEM((nchunk, KV_HEADS * rows, K_CHUNK), F32)]
        + [pltpu.VMEM((KV_HEADS * rows, LANES), F32)] * 4,
        compiler_params=pltpu.CompilerParams(dimension_semantics=("arbitrary", "arbitrary"),
                                             vmem_limit_bytes=VMEM_LIMIT),
    )(q, gates, ckv_perm, kvsb.reshape(batch, seq, 2 * KV_W), kvwb.reshape(batch, seq, 2 * KV_W), eneg)


def _prompt_group(x_prompt, g_pre_mix, g_post_mix, g_pre_ffn, g_post_ffn, wts):
    batch, seq, _ = x_prompt.shape
    x2 = x_prompt.reshape(batch * seq, D_MODEL)
    q, kvc, kvs, kvw, kvsb, kvwb, gates, pool_y, ulast, ckv = _inproj_prompt(x2, g_pre_mix, wts, seq, batch)
    ncb = seq // L_CMP
    ckv_perm = ckv.reshape(batch, ncb // 2, 2, 2 * KV_W).transpose(0, 2, 1, 3).reshape(batch, ncb, 2 * KV_W)
    o_att = _attn_prompt(q, gates, ckv_perm, kvsb, kvwb, seq, batch)
    y = _post(x2, o_att, pool_y, g_post_mix, g_pre_ffn, g_post_ffn, wts)
    wkeep = min(WINDOW, seq)
    kv6 = lambda a: jnp.transpose(a.reshape(1, batch, 2, KV_HEADS, HEAD_DIM, a.shape[-1]), (0, 1, 5, 2, 3, 4))
    return (y.reshape(batch, seq, D_MODEL), kv6(kvc), kv6(kvs), kv6(kvw[:, :, seq - wkeep:]),
            ulast[None, :, 1:, :])


def _inproj_sample_kernel(x_ref, g_ref, w_ref, rc_ref, rs1_ref, rs2_ref, cp_ref, wc_ref, cc_ref, cs1_ref, cs2_ref,
                          wp_ref, ps_ref, hist_ref,
                          q_ref, kvc_ref, kvs_ref, kvw_ref, gates_ref, pool_ref, znew_ref, ckv_ref, z_scr, *, past):
    nseq, zrows, _ = z_scr.shape
    t_new = zrows - HIST_ROWS
    n = nseq * t_new
    q, kvc, kvs, kvw, u, gates = _project_rows(x_ref[...], g_ref[...], w_ref, rc_ref[...], rs1_ref[...],
                                               rs2_ref[...])
    q_ref[...] = q
    kvc_ref[...] = kvc
    kvs_ref[...] = kvs
    kvw_ref[...] = kvw
    gates_ref[...] = gates

    summ = jnp.sum(kvc.reshape(nseq, t_new, 2 * KV_W) * cp_ref[0:t_new, :][None], axis=1)
    ckv = _dot(summ.astype(MXU_DTYPE), wc_ref[...])
    ckv_ref[...] = jnp.concatenate([_rope128(ckv[:, :KV_W], cc_ref[...], cs1_ref[...], cs2_ref[...]),
                                    ckv[:, KV_W:]], axis=1)

    z_scr[:, 0:HIST_ROWS, :] = hist_ref[...]
    z_scr[:, HIST_ROWS:, :] = u.reshape(nseq, t_new, POOL_W)
    zf = z_scr[...].reshape(nseq * zrows, POOL_W)
    tok = lax.broadcasted_iota(jnp.int32, (n, 1), 0) % t_new
    pos = (past + tok).astype(F32)
    cnt = [jnp.minimum(pos + 1.0, float(w)) for w in POOL_WINDOWS]
    take = lambda a: a.reshape(nseq, zrows, POOL_GC)[:, HIST_ROWS:, :].reshape(n, POOL_GC)
    d_list = _pool_windows(zf, take, cnt)
    pool_ref[...] = _pool_project(d_list, wp_ref, ps_ref[...]).astype(pool_ref.dtype)
    znew_ref[...] = z_scr[:, zrows - HIST_ROWS:, :]


def _inproj_sample(x2, g_pre, wts, hist16, nseq, t_new, past):
    n = x2.shape[0]
    rc, rs1, rs2 = _rope_tables(np.tile(past + np.arange(t_new), nseq))
    cc, cs1, cs2 = _rope_tables(np.array([past + L_CMP - 1]))
    full = lambda *s: pl.BlockSpec(s, lambda i: (0,) * len(s))
    out_shape = (
        jax.ShapeDtypeStruct((n, ATT_W), F32),
        jax.ShapeDtypeStruct((n, 2 * KV_W), F32),
        jax.ShapeDtypeStruct((n, 2 * KV_W), F32),
        jax.ShapeDtypeStruct((n, 2 * KV_W), F32),
        jax.ShapeDtypeStruct((n, LANES), F32),
        jax.ShapeDtypeStruct((n, POOL_W), MXU_DTYPE),
        jax.ShapeDtypeStruct((nseq, HIST_ROWS, POOL_W), F32),
        jax.ShapeDtypeStruct((nseq, 2 * KV_W), F32),
    )
    out_specs = (full(n, ATT_W), full(n, 2 * KV_W), full(n, 2 * KV_W), full(n, 2 * KV_W), full(n, LANES),
                 full(n, POOL_W), full(nseq, HIST_ROWS, POOL_W), full(nseq, 2 * KV_W))
    in_specs = [full(n, D_MODEL), full(1, D_MODEL), full(D_MODEL, IN_W_PAD), full(n, LANES), full(n, LANES),
                full(n, LANES), full(L_CMP, 2 * KV_W), full(2 * KV_W, 2 * KV_W), full(1, LANES), full(1, LANES),
                full(1, LANES), full(len(POOL_WINDOWS), POOL_GC, POOL_GC), full(1, POOL_W),
                full(nseq, HIST_ROWS, POOL_W)]
    return pl.pallas_call(
        functools.partial(_inproj_sample_kernel, past=past),
        grid=(1,),
        in_specs=in_specs, out_specs=out_specs, out_shape=out_shape,
        scratch_shapes=[pltpu.VMEM((nseq, HIST_ROWS + t_new, POOL_W), F32)],
        compiler_params=pltpu.CompilerParams(dimension_semantics=("arbitrary",), vmem_limit_bytes=VMEM_LIMIT),
    )(x2, g_pre.reshape(1, D_MODEL), wts["w_in"], rc, rs1, rs2, wts["cp"], wts["wc"], cc, cs1, cs2,
      wts["w_pool"], wts["pool_scale"], hist16)


PAGE_CHUNK = 16
CMP_LANE_BLKS = PAGE_SIZE // L_CMP
HALF_PAGES = LANES // CMP_LANE_BLKS
CMP_PAGES_PER_STEP = 8


def _sample_tables(past, t_new, wb):
    npages = past // PAGE_SIZE
    nrow = HPG * KV_HEADS * t_new
    r = np.arange(nrow)
    t_row = r % t_new
    pos = past + t_row
    lanes = np.arange(npages * CMP_LANE_BLKS)
    page = HALF_PAGES * (lanes // LANES) + lanes % HALF_PAGES
    m = (lanes % LANES) // HALF_PAGES
    blk = CMP_LANE_BLKS * page + (m - 1) % CMP_LANE_BLKS
    cmask = (blk[None, :] + 1) * L_CMP - 1 <= pos[:, None]
    n_new = 2 * t_new
    blk_new = past // L_CMP + np.arange(n_new)
    n_cmp = (past + -(-t_new // L_SEL) * L_SEL) // L_CMP
    cmask_new = ((blk_new[None, :] + 1) * L_CMP - 1 <= pos[:, None]) & (blk_new[None, :] < n_cmp)
    jidx = np.concatenate([2 * np.arange(npages), 2 * np.arange(npages) + 1])
    rg = np.arange(KV_HEADS * t_new)
    c = ((past + rg % t_new) // L_SEL)[:, None]
    jj = jidx[None, :]
    cand = (jj >= 1) & (jj <= c - 2)
    forced = (jj == 0) | (jj == c) | (jj == c - 1)
    l = np.arange(n_new)
    new_blk = past // L_SEL
    c_row = (pos // L_SEL)[:, None]
    new_forced = (new_blk == 0) | (new_blk == c_row) | (new_blk == c_row - 1)
    nmask = new_forced & (l[None, :] <= t_row[:, None]) & (l[None, :] < t_new)
    i = np.arange(wb)
    d = t_row[:, None] + wb - i[None, :]
    wmask = (d >= 0) & (d < WINDOW) & (past - wb + i[None, :] >= 0)
    dn = t_row[:, None] - l[None, :]
    wmask_new = (dn >= 0) & (dn < WINDOW) & (l[None, :] < t_new)
    ck = PAGE_CHUNK * PAGE_SIZE
    k = np.arange(ck)
    eeo = np.zeros((npages // PAGE_CHUNK, 2 * npages, ck), np.float32)
    for ch in range(npages // PAGE_CHUNK):
        pg = PAGE_CHUNK * ch + k // PAGE_SIZE
        upper = (k % PAGE_SIZE) >= L_SEL
        eeo[ch, pg[~upper], k[~upper]] = 1.0
        eeo[ch, npages + pg[upper], k[upper]] = 1.0
    tt = np.arange(PAGE_SIZE)
    steps = HALF_PAGES // CMP_PAGES_PER_STEP
    gsum = np.zeros((steps, CMP_PAGES_PER_STEP * PAGE_SIZE, LANES), np.float32)
    for it in range(steps):
        for kk in range(CMP_PAGES_PER_STEP):
            col = HALF_PAGES * ((tt // L_CMP + 1) % CMP_LANE_BLKS) + it * CMP_PAGES_PER_STEP + kk
            gsum[it, kk * PAGE_SIZE + tt, col] = 1.0
    f = lambda a: jnp.asarray(np.asarray(a, np.float32))
    return dict(gsum=jnp.asarray(gsum, MXU_DTYPE), cmask=f(cmask), cmask_new=f(cmask_new), cand=f(cand), forced=f(forced), nmask=f(nmask),
                wmask=f(wmask), wmask_new=f(wmask_new), jidx=jnp.asarray(jidx[None, :], jnp.int32),
                eeo=jnp.asarray(eeo, MXU_DTYPE), blk=blk, jidx_np=jidx)


def _rope_tables_t(pos):
    c, s1, s2 = _rope_tables(pos)
    return c.T, s1.T, s2.T


def _attn_sample_kernel(pt_ref, q_ref, gt_ref, kvs_ref, kvw_ref, ckv0_ref, cw_ref, cmp_hbm, sel_hbm,
                        cpt_ref, wct_ref, cc_ref, cs1_ref, cs2_ref, cmask_ref, cmaskn_ref, cand_ref, forced_ref,
                        nmask_ref, wmask_ref, wmaskn_ref, jidx_ref, eeo_ref, gsum_ref,
                        o_ref, wout_ref,
                        cbuf, sbuf, sem, summ_scr, s_scr, *, jidx_np, n_top):
    b = pl.program_id(0)
    nseq = pl.num_programs(0)
    npages = pt_ref.shape[1]
    t_new = q_ref.shape[1]
    nrow = HPG * KV_HEADS * t_new
    ngt = KV_HEADS * t_new
    slot = b % 2

    def copies(seq, sl):
        out = []
        for p in range(npages):
            pg = pt_ref[seq, p]
            out.append(pltpu.make_async_copy(cmp_hbm.at[pg], cbuf.at[sl, p], sem.at[0, sl]))
            out.append(pltpu.make_async_copy(sel_hbm.at[pg], sbuf.at[sl, p], sem.at[1, sl]))
        return out

    @pl.when(b == 0)
    def _():
        for cpy in copies(b, slot):
            cpy.start()

    @pl.when(b + 1 < nseq)
    def _():
        for cpy in copies(b + 1, 1 - slot):
            cpy.start()

    qb = q_ref[0]
    zeros = jnp.zeros((t_new, HEAD_DIM), F32)
    blocks = []
    for h in range(HPG):
        for g in range(KV_HEADS):
            piece = qb[:, (g * HPG + h) * HEAD_DIM:(g * HPG + h + 1) * HEAD_DIM]
            blocks.append(jnp.concatenate([piece, zeros] if g == 0 else [zeros, piece], axis=1))
    qbd = jnp.concatenate(blocks, axis=0).astype(MXU_DTYPE)

    def new_rows(rows):
        a = jnp.concatenate([rows, jnp.zeros((2 * t_new - rows.shape[0], 2 * KV_W), F32)], axis=0)
        return a[:, :KV_W].astype(MXU_DTYPE), a[:, KV_W:].astype(MXU_DTYPE)

    def joint_softmax(s1, ok1, s2, ok2):
        s1 = jnp.where(ok1, s1, NEG)
        s2 = jnp.where(ok2, s2, NEG)
        m = jnp.maximum(jnp.max(s1, axis=1, keepdims=True), jnp.max(s2, axis=1, keepdims=True))
        e1 = jnp.where(ok1, jnp.exp(s1 - m), 0.0)
        e2 = jnp.where(ok2, jnp.exp(s2 - m), 0.0)
        l = jnp.maximum(jnp.sum(e1, axis=1, keepdims=True) + jnp.sum(e2, axis=1, keepdims=True), 1e-30)
        return e1 / l, e2 / l

    kwn, vwn = new_rows(kvw_ref[0])
    p1, p2 = joint_softmax(_dot(qbd, cw_ref[0, 0:KV_W, :].astype(MXU_DTYPE)), wmask_ref[...] > 0.5,
                           _dot_nt(qbd, kwn), wmaskn_ref[...] > 0.5)
    o_w = (_dot_nt(p1.astype(MXU_DTYPE), cw_ref[0, KV_W:2 * KV_W, :].astype(MXU_DTYPE))
           + _dot(p2.astype(MXU_DTYPE), vwn))

    wb = cw_ref.shape[2]
    shifted = pltpu.roll(cw_ref[0], wb - t_new, 1)
    new_t = jnp.concatenate([jnp.zeros((LANES - t_new, 2 * KV_W), F32), kvw_ref[0]], axis=0).T
    wlane = lax.broadcasted_iota(jnp.int32, (2 * KV_W, LANES), 1)
    wout_ref[0, :, 0:wb - LANES] = shifted[:, 0:wb - LANES]
    wout_ref[0, :, wb - LANES:wb] = jnp.where(wlane >= LANES - t_new, new_t, shifted[:, wb - LANES:wb])

    for cpy in copies(b, slot):
        cpy.wait()

    nhalf = npages // HALF_PAGES
    summ_scr[...] = jnp.zeros(summ_scr.shape, F32)

    def cmp_body(it, _):
        xs = []
        for k in range(CMP_PAGES_PER_STEP):
            pg = it * CMP_PAGES_PER_STEP + k
            x = jnp.concatenate([cbuf[slot, h2 * HALF_PAGES + pg] * cpt_ref[...] for h2 in range(nhalf)], axis=0)
            xs.append(x.astype(MXU_DTYPE))
        summ_scr[...] += _dot(jnp.concatenate(xs, axis=1), gsum_ref[it])
        return 0

    lax.fori_loop(0, HALF_PAGES // CMP_PAGES_PER_STEP, cmp_body, 0)
    summ = jnp.concatenate([summ_scr[h2 * 2 * KV_W:(h2 + 1) * 2 * KV_W] for h2 in range(nhalf)], axis=1)
    ckv = _dot(wct_ref[...], summ.astype(MXU_DTYPE))
    kc = ckv[0:KV_W]
    ckt = (kc * cc_ref[...] + pltpu.roll(kc, ROT_HALF, 0) * cs1_ref[...]
           + pltpu.roll(kc, KV_W - ROT_HALF, 0) * cs2_ref[...]).astype(MXU_DTYPE)
    cvt = ckv[KV_W:2 * KV_W].astype(MXU_DTYPE)

    ckn, cvn = new_rows(ckv0_ref[0])
    p1, p2 = joint_softmax(_dot(qbd, ckt), cmask_ref[...] > 0.5, _dot_nt(qbd, ckn), cmaskn_ref[...] > 0.5)
    o_c = _dot_nt(p1.astype(MXU_DTYPE), cvt) + _dot(p2.astype(MXU_DTYPE), cvn)
    pc = p1[0:ngt]
    for h in range(1, HPG):
        pc = pc + p1[h * ngt:(h + 1) * ngt]
    ntile = pc.shape[1] // LANES
    pair = [pc[:, t * LANES:(t + 1) * LANES] + pltpu.roll(pc[:, t * LANES:(t + 1) * LANES], LANES - HALF_PAGES, 1)
            for t in range(ntile)]
    pair = jnp.concatenate([pr[:, HALF_PAGES:2 * HALF_PAGES] for pr in pair]
                           + [pr[:, 3 * HALF_PAGES:LANES] for pr in pair], axis=1)

    cand = cand_ref[...] > 0.5
    sc = jnp.where(cand, pair, -1.0)
    jl = jidx_ref[...]
    rank = jnp.zeros(sc.shape, F32)
    for i in range(sc.shape[1]):
        col = sc[:, i:i + 1]
        beats = (col > sc) | ((col == sc) & (jl > int(jidx_np[i])))
        rank = rank + jnp.where(beats, 1.0, 0.0)
    sel_eo = jnp.where((cand & (rank < n_top)) | (forced_ref[...] > 0.5), 1.0, 0.0).astype(MXU_DTYPE)

    def score_body(ch, mel):
        keep = _dot(sel_eo, eeo_ref[ch]) > 0.5
        bias = jnp.where(keep, 0.0, NEG)
        bias = jnp.concatenate([bias] * HPG, axis=0)
        for pp in range(0, PAGE_CHUNK, 2):
            pg = ch * PAGE_CHUNK + pp
            kt = jnp.concatenate([sbuf[slot, pg, 0:KV_W, :], sbuf[slot, pg + 1, 0:KV_W, :]], axis=1)
            sv = _dot(qbd, kt.astype(MXU_DTYPE)) + bias[:, pp * LANES:(pp + 2) * LANES]
            s_scr[pg] = sv[:, 0:LANES]
            s_scr[pg + 1] = sv[:, LANES:2 * LANES]
            mel = jnp.maximum(mel, jnp.maximum(sv[:, 0:LANES], sv[:, LANES:2 * LANES]))
        return mel

    mel = lax.fori_loop(0, npages // PAGE_CHUNK, score_body, jnp.full((nrow, LANES), NEG, F32))
    ksn, vsn = new_rows(kvs_ref[0])
    nok = nmask_ref[...] > 0.5
    s_new = jnp.where(nok, _dot_nt(qbd, ksn), NEG)
    m = jnp.maximum(jnp.max(mel, axis=1, keepdims=True), jnp.max(s_new, axis=1, keepdims=True))
    mb = jnp.broadcast_to(m, (nrow, LANES))

    def value_body(ch, carry):
        lel, acc = carry
        for pp in range(0, PAGE_CHUNK, 2):
            pg = ch * PAGE_CHUNK + pp
            pv0 = jnp.exp(s_scr[pg] - mb)
            pv1 = jnp.exp(s_scr[pg + 1] - mb)
            lel = lel + (pv0 + pv1)
            vt = jnp.concatenate([sbuf[slot, pg, KV_W:2 * KV_W, :], sbuf[slot, pg + 1, KV_W:2 * KV_W, :]], axis=1)
            acc = acc + _dot_nt(jnp.concatenate([pv0, pv1], axis=1).astype(MXU_DTYPE), vt.astype(MXU_DTYPE))
        return lel, acc

    e_new = jnp.where(nok, jnp.exp(s_new - m), 0.0)
    lel, acc = lax.fori_loop(0, npages // PAGE_CHUNK, value_body,
                             (jnp.zeros((nrow, LANES), F32), _dot(e_new.astype(MXU_DTYPE), vsn)))
    l = jnp.sum(lel, axis=1, keepdims=True) + jnp.sum(e_new, axis=1, keepdims=True)
    o_s = acc / jnp.maximum(l, 1e-30)

    gs = jax.nn.sigmoid(gt_ref[0])
    outs = [None] * N_HEADS
    for h in range(HPG):
        for g in range(KV_HEADS):
            hh = g * HPG + h
            rs = slice((h * KV_HEADS + g) * t_new, (h * KV_HEADS + g + 1) * t_new)
            cs = slice(g * HEAD_DIM, (g + 1) * HEAD_DIM)
            col = hh * N_BRANCH
            outs[hh] = (gs[:, col:col + 1] * o_c[rs, cs] + gs[:, col + 1:col + 2] * o_s[rs, cs]
                        + gs[:, col + 2:col + 3] * o_w[rs, cs])
    o_ref[0] = jnp.concatenate(outs, axis=1)


def _dims_major(a):
    n, toks = a.shape[:2]
    return jnp.transpose(a, (0, 2, 3, 4, 1)).reshape(n, 2 * KV_W, toks)


def _attn_sample(q, gates, kvs, kvw, ckv0, cache_cmp, cache_sel, cache_win, page_table, wts, t_new):
    nseq, npages = page_table.shape
    past = npages * PAGE_SIZE
    wb = cache_win.shape[1]
    nrow = HPG * KV_HEADS * t_new
    ngt = KV_HEADS * t_new
    nlane = npages * CMP_LANE_BLKS
    assert t_new <= L_CMP and past % L_SEL == 0 and wb == WINDOW and wb % LANES == 0
    assert npages % HALF_PAGES == 0 and npages % PAGE_CHUNK == 0 and 2 * npages == LANES
    tb = _sample_tables(past, t_new, wb)
    cc, cs1, cs2 = _rope_tables_t((tb["blk"] + 1) * L_CMP - 1)
    cpt = jnp.tile(wts["cp"].T, (1, CMP_LANE_BLKS))
    seq3 = lambda r, w: pl.BlockSpec((1, r, w), lambda b, pt: (b, 0, 0))
    const = lambda *s: pl.BlockSpec(s, lambda b, pt: (0,) * len(s), pipeline_mode=pl.Buffered(1))
    hbm = pl.BlockSpec(memory_space=pl.ANY)
    in_specs = [seq3(t_new, ATT_W), seq3(t_new, LANES), seq3(t_new, 2 * KV_W), seq3(t_new, 2 * KV_W),
                seq3(1, 2 * KV_W), seq3(2 * KV_W, wb), hbm, hbm,
                const(2 * KV_W, PAGE_SIZE), const(2 * KV_W, 2 * KV_W), const(KV_W, nlane), const(KV_W, nlane),
                const(KV_W, nlane), const(nrow, nlane), const(nrow, 2 * t_new), const(ngt, 2 * npages),
                const(ngt, 2 * npages), const(nrow, 2 * t_new), const(nrow, wb), const(nrow, 2 * t_new),
                const(1, 2 * npages), const(npages // PAGE_CHUNK, 2 * npages, PAGE_CHUNK * PAGE_SIZE),
                const(HALF_PAGES // CMP_PAGES_PER_STEP, CMP_PAGES_PER_STEP * PAGE_SIZE, LANES)]
    grid_spec = pltpu.PrefetchScalarGridSpec(
        num_scalar_prefetch=1, grid=(nseq,), in_specs=in_specs,
        out_specs=(seq3(t_new, ATT_W), seq3(2 * KV_W, wb)),
        scratch_shapes=[pltpu.VMEM((2, npages, 2 * KV_W, PAGE_SIZE), F32),
                        pltpu.VMEM((2, npages, 2 * KV_W, PAGE_SIZE), F32),
                        pltpu.SemaphoreType.DMA((2, 2)),
                        pltpu.VMEM((nlane // LANES * 2 * KV_W, LANES), F32),
                        pltpu.VMEM((npages, nrow, PAGE_SIZE), F32)])
    return pl.pallas_call(
        functools.partial(_attn_sample_kernel, jidx_np=tb["jidx_np"], n_top=N_TOP),
        grid_spec=grid_spec,
        out_shape=(jax.ShapeDtypeStruct((nseq, t_new, ATT_W), F32),
                   jax.ShapeDtypeStruct((nseq, 2 * KV_W, wb), F32)),
        compiler_params=pltpu.CompilerParams(dimension_semantics=("arbitrary",),
                                             vmem_limit_bytes=SAMPLE_VMEM_LIMIT),
    )(page_table, q.reshape(nseq, t_new, ATT_W), gates.reshape(nseq, t_new, LANES),
      kvs.reshape(nseq, t_new, 2 * KV_W), kvw.reshape(nseq, t_new, 2 * KV_W), ckv0.reshape(nseq, 1, 2 * KV_W),
      _dims_major(cache_win), _dims_major(cache_cmp), _dims_major(cache_sel), cpt, wts["wc"].T, cc, cs1, cs2,
      tb["cmask"], tb["cmask_new"], tb["cand"], tb["forced"], tb["nmask"], tb["wmask"], tb["wmask_new"],
      tb["jidx"], tb["eeo"], tb["gsum"])


def _sample_group(x_sample, cache_cmp, cache_sel, cache_win, state_pool, page_table, g_pre_mix, g_post_mix,
                  g_pre_ffn, g_post_ffn, wts):
    nseq, t_new, _ = x_sample.shape
    past = page_table.shape[1] * PAGE_SIZE
    wb = cache_win.shape[1]
    x2 = x_sample.reshape(nseq * t_new, D_MODEL)
    hist16 = jnp.pad(state_pool, ((0, 0), (HIST_ROWS - POOL_HIST, 0), (0, 0)))
    q, kvc, kvs, kvw, gates, pool_y, znew, ckv0 = _inproj_sample(x2, g_pre_mix, wts, hist16, nseq, t_new, past)
    o_att, win_t = _attn_sample(q, gates, kvs, kvw, ckv0, cache_cmp, cache_sel, cache_win, page_table, wts, t_new)
    y = _post(x2, o_att.reshape(nseq * t_new, ATT_W), pool_y, g_post_mix, g_pre_ffn, g_post_ffn, wts)
    kv6 = lambda a: a.reshape(1, nseq, t_new, 2, KV_HEADS, HEAD_DIM)
    win_new = jnp.transpose(win_t.reshape(1, nseq, 2, KV_HEADS, HEAD_DIM, wb), (0, 1, 5, 2, 3, 4))
    return (y.reshape(nseq, t_new, D_MODEL), kv6(kvc), kv6(kvs), win_new, znew[None, :, 1:, :])


def kernel(x_prompt, x_sample, cache_kv_cmp, cache_kv_sel, cache_kv_win, state_pool, page_table, g_pre_mix,
           g_post_mix, g_pre_ffn, g_post_ffn, w_in, cmp_pos, w_cmp, w_pool, pool_scale, w_out, w_gate, w_up, w_down):
    wts = _prep_weights(w_in[0], cmp_pos[0], w_cmp[0], w_pool[0], pool_scale[0], w_out[0], w_gate[0], w_up[0],
                        w_down[0])
    yp, cmp_p, sel_p, win_p, pool_p = _prompt_group(x_prompt, g_pre_mix[0], g_post_mix[0], g_pre_ffn[0],
                                                    g_post_ffn[0], wts)
    ys, cmp_s, sel_s, win_s, pool_s = _sample_group(x_sample, cache_kv_cmp[0], cache_kv_sel[0], cache_kv_win[0],
                                                    state_pool[0], page_table, g_pre_mix[0], g_post_mix[0],
                                                    g_pre_ffn[0], g_post_ffn[0], wts)
    return (yp, ys, cmp_p, sel_p, win_p, pool_p, cmp_s, sel_s, win_s, pool_s)
```
